```python
import math
import jax, jax.numpy as jnp
from jax import lax
import numpy as np

D_MODEL = 1024
BATCH = 16
SEQ = 2048
DEPTH = 1

HEAD_DIM = 64
D_MIX = D_MODEL
A_GROUPS = 4
A_WIDTH = A_GROUPS * HEAD_DIM
B_HEADS = 12
B_WIDTH = B_HEADS * HEAD_DIM
CHUNK = 128
DILATED_CONFIGS = ((128, 1), (512, 4), (2048, 16))
NUM_BUCKETS = 32
MAX_DISTANCE = 2048
D_FF = 2816
CONV_WIDTH = 3
IN_COLS = 2 * A_WIDTH + 3 * B_WIDTH
NORM_EPS = 1e-6
NEG_INF = -1e30

kernel_name = "hybrid_gmlp_dilated_attn_convffn"


def rms_norm(x, g):
    xf = x.astype(jnp.float32)
    y = xf * lax.rsqrt(jnp.mean(xf * xf, axis=-1, keepdims=True) + NORM_EPS)
    return (y * g.astype(jnp.float32)).astype(x.dtype)


def t5_bucket(dist):
    max_exact = NUM_BUCKETS // 2
    d = jnp.maximum(dist, 1).astype(jnp.float32)
    large = max_exact + (jnp.log(d / max_exact) / math.log(MAX_DISTANCE / max_exact)
                         * (NUM_BUCKETS - max_exact))
    large = jnp.minimum(large.astype(jnp.int32), NUM_BUCKETS - 1)
    return jnp.where(dist < max_exact, dist, large)


def spatial_gating(u, v, ln_g, ln_b, w_s, b_s):
    B, T, G, hd = u.shape
    u = jax.nn.gelu(u)
    vf = jax.nn.gelu(v).astype(jnp.float32)
    mu = jnp.mean(vf, axis=-1, keepdims=True)
    var = jnp.mean(jnp.square(vf - mu), axis=-1, keepdims=True)
    vn = (vf - mu) * lax.rsqrt(var + NORM_EPS) * ln_g.astype(jnp.float32) + ln_b.astype(jnp.float32)
    vc = vn.reshape(B, T // CHUNK, CHUNK, G, hd)
    tril = jnp.tril(jnp.ones((CHUNK, CHUNK), jnp.float32))
    w = w_s.astype(jnp.float32) * tril[None]
    z = jnp.einsum('gts,bcsgd->bctgd', w, vc) + b_s.astype(jnp.float32).T[None, None, :, :, None]
    return u * z.reshape(B, T, G, hd).astype(u.dtype)


def dilated_segment(q, k, v, rel_bias, window, dil):
    B, T, H, D = q.shape
    nw = window // dil
    seg = nw * dil
    nb = -(-T // seg)
    Tp = nb * seg

    def blocks(a):
        a = jnp.pad(a, ((0, 0), (0, Tp - T), (0, 0), (0, 0)))
        return a.reshape(B, nb, nw, dil, H, D)

    def with_prev(a):
        prev = jnp.pad(a, ((0, 0), (1, 0), (0, 0), (0, 0), (0, 0), (0, 0)))[:, :-1]
        return jnp.concatenate([prev, a], axis=2)

    qb = blocks(q)
    kc = with_prev(blocks(k))
    vc = with_prev(blocks(v))

    i = jnp.arange(nw)[:, None]
    j = jnp.arange(2 * nw)[None, :]
    rel = nw + i - j
    band = (rel >= 0) & (rel <= nw)
    key_ok = (jnp.arange(nb)[:, None] * nw + jnp.arange(2 * nw)[None, :] - nw) >= 0
    mask = band[None] & key_ok[:, None, :]
    bias = rel_bias.astype(jnp.float32)[t5_bucket(jnp.maximum(rel, 0) * dil)]
    bias = bias.transpose(2, 0, 1)

    scale = 1.0 / math.sqrt(D)
    logits = jnp.einsum('bnirhd,bnjrhd->bnrhij', qb, kc) * scale + bias[None, None, None]
    logits = jnp.where(mask[None, :, None, None], logits, NEG_INF)
    lse = jax.nn.logsumexp(logits, axis=-1)
    p = jnp.exp(logits - lse[..., None])
    o = jnp.einsum('bnrhij,bnjrhd->bnirhd', p, vc)
    o = o.reshape(B, Tp, H, D)[:, :T]
    lse = lse.transpose(0, 1, 4, 2, 3).reshape(B, Tp, H)[:, :T]
    return o, lse


def dilated_attention(q, k, v, rel_bias):
    outs, lses = [], []
    for window, dil in DILATED_CONFIGS:
        o, lse = dilated_segment(q, k, v, rel_bias, window, dil)
        outs.append(o)
        lses.append(lse)
    o = jnp.stack(outs, axis=0)
    w = jax.nn.softmax(jnp.stack(lses, axis=0), axis=0)
    return jnp.sum(w[..., None] * o, axis=0)


def causal_dwconv(h, w, b):
    T = h.shape[1]
    hp = jnp.pad(h, ((0, 0), (CONV_WIDTH - 1, 0), (0, 0)))
    out = b
    for kk in range(CONV_WIDTH):
        out = out + hp[:, kk:kk + T] * w[kk]
    return out


def setup_inputs(seed: int = 0) -> dict:
    key = jax.random.key(seed)
    ks = jax.random.split(key, 20)
    f32 = jnp.float32

    def nrm(k, shape, s):
        return jax.random.normal(k, shape, f32) * s

    L = DEPTH
    return {
        "x": jax.random.normal(ks[0], (BATCH, SEQ, D_MODEL), f32),
        "norm_mix_pre": 1.0 + nrm(ks[1], (L, D_MODEL), 0.01),
        "norm_mix_post": 1.0 + nrm(ks[2], (L, D_MODEL), 0.01),
        "norm_ffn_pre": 1.0 + nrm(ks[3], (L, D_MODEL), 0.01),
        "norm_ffn_post": 1.0 + nrm(ks[4], (L, D_MODEL), 0.01),
        "w_in": nrm(ks[5], (L, D_MODEL, IN_COLS), D_MODEL ** -0.5),
        "ln_v_gain": 1.0 + nrm(ks[6], (L, A_GROUPS, HEAD_DIM), 0.01),
        "ln_v_bias": nrm(ks[7], (L, A_GROUPS, HEAD_DIM), 0.01),
        "spatial_w": nrm(ks[8], (L, A_GROUPS, CHUNK, CHUNK), CHUNK ** -0.5),
        "spatial_b": 1.0 + nrm(ks[9], (L, A_GROUPS, CHUNK), 0.01),
        "rel_bias": nrm(ks[10], (NUM_BUCKETS, B_HEADS), 0.5),
        "w_out": nrm(ks[11], (L, D_MIX, D_MODEL), D_MIX ** -0.5),
        "w_gate": nrm(ks[12], (L, D_MODEL, D_FF), D_MODEL ** -0.5),
        "w_up": nrm(ks[13], (L, D_MODEL, D_FF), D_MODEL ** -0.5),
        "conv_w": nrm(ks[14], (L, CONV_WIDTH, D_FF), CONV_WIDTH ** -0.5),
        "conv_b": nrm(ks[15], (L, D_FF), 0.01),
        "w_down": nrm(ks[16], (L, D_FF, D_MODEL), D_FF ** -0.5),
    }


def reference(x, norm_mix_pre, norm_mix_post, norm_ffn_pre, norm_ffn_post, w_in,
              ln_v_gain, ln_v_bias, spatial_w, spatial_b, rel_bias, w_out,
              w_gate, w_up, conv_w, conv_b, w_down):
    B, T, _ = x.shape
    for l in range(DEPTH):
        h = rms_norm(x, norm_mix_pre[l])
        proj = h @ w_in[l]
        o0 = 0
        ua = proj[..., o0:o0 + A_WIDTH].reshape(B, T, A_GROUPS, HEAD_DIM); o0 += A_WIDTH
        va = proj[..., o0:o0 + A_WIDTH].reshape(B, T, A_GROUPS, HEAD_DIM); o0 += A_WIDTH
        qb = proj[..., o0:o0 + B_WIDTH].reshape(B, T, B_HEADS, HEAD_DIM); o0 += B_WIDTH
        kb = proj[..., o0:o0 + B_WIDTH].reshape(B, T, B_HEADS, HEAD_DIM); o0 += B_WIDTH
        vb = proj[..., o0:o0 + B_WIDTH].reshape(B, T, B_HEADS, HEAD_DIM)

        a_out = spatial_gating(ua, va, ln_v_gain[l], ln_v_bias[l], spatial_w[l], spatial_b[l])
        b_out = dilated_attention(qb.astype(jnp.float32), kb.astype(jnp.float32),
                                  vb.astype(jnp.float32), rel_bias)
        mix = jnp.concatenate([a_out.reshape(B, T, A_WIDTH),
                               b_out.reshape(B, T, B_WIDTH).astype(x.dtype)], axis=-1)
        x = x + rms_norm(mix @ w_out[l], norm_mix_post[l])

        h = rms_norm(x, norm_ffn_pre[l])
        g = jax.nn.gelu(causal_dwconv(h @ w_gate[l], conv_w[l], conv_b[l]))
        y = (g * (h @ w_up[l])) @ w_down[l]
        x = x + rms_norm(y, norm_ffn_post[l])
    return x
```

```python
import functools
import math

import numpy as np
import jax
import jax.numpy as jnp
from jax import lax
from jax.experimental import pallas as pl
from jax.experimental.pallas import tpu as pltpu

F32 = jnp.float32
BF16 = jnp.bfloat16

HEAD_DIM = 64
A_GROUPS = 4
A_WIDTH = A_GROUPS * HEAD_DIM
B_HEADS = 12
B_WIDTH = B_HEADS * HEAD_DIM
CHUNK = 128
DILATIONS = (1, 4, 16)
KEYS_BACK = 128
NUM_BUCKETS = 32
MAX_DISTANCE = 2048
CONV_WIDTH = 3
NORM_EPS = 1e-6
NEG_INF = -1e30
LOG2E = 1.4426950408889634

LANES = 128
HEADS_PER_STEP = LANES // HEAD_DIM
FF_CHUNK = 256
TOKENS_PER_STEP = 512
VMEM_LIMIT = 56 * 1024 * 1024


def _rms(x, g):
    ms = jnp.mean(x * x, axis=-1, keepdims=True)
    return x * lax.rsqrt(ms + NORM_EPS) * g


def _dot(a, b):
    return jnp.dot(a, b, preferred_element_type=F32)


def _proj_kernel(x_ref, g_ref, w_ref, gmat_ref, lng_ref, lnb_ref, sw_ref, sb_ref,
                 a_ref, q_ref, k_ref, v_ref):
    tm = x_ref.shape[0]
    h = _rms(x_ref[...], g_ref[...]).astype(BF16)

    o = 2 * A_WIDTH
    q_ref[...] = (_dot(h, w_ref[:, o:o + B_WIDTH]) * (LOG2E / math.sqrt(HEAD_DIM))).astype(BF16)
    k_ref[...] = _dot(h, w_ref[:, o + B_WIDTH:o + 2 * B_WIDTH]).astype(BF16)
    v_ref[...] = _dot(h, w_ref[:, o + 2 * B_WIDTH:o + 3 * B_WIDTH]).astype(BF16)

    uv = _dot(h, w_ref[:, 0:o])
    u = jax.nn.gelu(uv[:, :A_WIDTH])
    vf = jax.nn.gelu(uv[:, A_WIDTH:])

    gmat = gmat_ref[...]

    def group_sum(t):
        hi = t.astype(BF16)
        lo = (t - hi.astype(F32)).astype(BF16)
        return _dot(hi, gmat) + _dot(lo, gmat)

    mu = group_sum(vf) * (1.0 / HEAD_DIM)
    dl = vf - mu
    var = group_sum(dl * dl) * (1.0 / HEAD_DIM)
    vn = (dl * lax.rsqrt(var + NORM_EPS) * lng_ref[...] + lnb_ref[...]).astype(BF16)

    row = lax.broadcasted_iota(jnp.int32, (CHUNK, CHUNK), 0)
    col = lax.broadcasted_iota(jnp.int32, (CHUNK, CHUNK), 1)
    tril = row >= col
    ws = [jnp.where(tril, sw_ref[g], 0.0).astype(BF16) for g in range(A_GROUPS)]
    grp = lax.broadcasted_iota(jnp.int32, (CHUNK, A_WIDTH), 1) // HEAD_DIM
    sb = sb_ref[...]
    for c in range(tm // CHUNK):
        rows = slice(c * CHUNK, (c + 1) * CHUNK)
        vc = vn[rows]
        z = _dot(ws[0], vc)
        for g in range(1, A_GROUPS):
            z = jnp.where(grp == g, _dot(ws[g], vc), z)
        a_ref[rows, :] = (u[rows] * (z + sb)).astype(BF16)


def _proj_call(x2, g_pre, w_in, gmat, lng, lnb, sw, sb):
    n, d = x2.shape
    tm = TOKENS_PER_STEP
    cols = w_in.shape[1]
    const = lambda i: (0, 0)
    return pl.pallas_call(
        _proj_kernel,
        grid=(n // tm,),
        in_specs=[
            pl.BlockSpec((tm, d), lambda i: (i, 0)),
            pl.BlockSpec((1, d), const),
            pl.BlockSpec((d, cols), const),
            pl.BlockSpec((A_WIDTH, A_WIDTH), const),
            pl.BlockSpec((1, A_WIDTH), const),
            pl.BlockSpec((1, A_WIDTH), const),
            pl.BlockSpec((A_GROUPS, CHUNK, CHUNK), lambda i: (0, 0, 0)),
            pl.BlockSpec((CHUNK, A_WIDTH), const),
        ],
        out_specs=[
            pl.BlockSpec((tm, A_WIDTH), lambda i: (i, 0)),
            pl.BlockSpec((tm, B_WIDTH), lambda i: (i, 0)),
            pl.BlockSpec((tm, B_WIDTH), lambda i: (i, 0)),
            pl.BlockSpec((tm, B_WIDTH), lambda i: (i, 0)),
        ],
        out_shape=[
            jax.ShapeDtypeStruct((n, A_WIDTH), BF16),
            jax.ShapeDtypeStruct((n, B_WIDTH), BF16),
            jax.ShapeDtypeStruct((n, B_WIDTH), BF16),
            jax.ShapeDtypeStruct((n, B_WIDTH), BF16),
        ],
        compiler_params=pltpu.CompilerParams(
            dimension_semantics=("arbitrary",), vmem_limit_bytes=VMEM_LIMIT),
        name="proj_gmlp",
    )(x2, g_pre, w_in, gmat, lng, lnb, sw, sb)


def _attn_kernel(q_ref, k_ref, v_ref, bias_ref, o_ref,
                 qst, kst, vst, q0, q1, kt, va, mb, lb, ab, um, ul, ua):
    seq = q_ref.shape[0]
    nblk = seq // KEYS_BACK
    lane = lax.broadcasted_iota(jnp.int32, (KEYS_BACK, LANES), 1)
    first_head = lane < HEAD_DIM

    qst[...] = q_ref[...].astype(F32)
    kst[...] = k_ref[...].astype(F32)
    vst[...] = v_ref[...].astype(F32)

    def src_rows(d, g):
        blocks_per_residue = nblk // d
        r, n = g // blocks_per_residue, g % blocks_per_residue
        if d == 1:
            return pl.ds(g * KEYS_BACK, KEYS_BACK)
        return pl.ds(r + n * KEYS_BACK * d, KEYS_BACK, stride=d)

    ones = jnp.ones((KEYS_BACK, LANES), BF16)
    for di, d in enumerate(DILATIONS):
        for g in range(nblk):
            src = src_rows(d, g)
            dst = pl.ds(g * KEYS_BACK, KEYS_BACK)
            qs = qst[src, :]
            q0[di, dst, :] = jnp.where(first_head, qs, 0.0).astype(BF16)
            q1[di, dst, :] = jnp.where(first_head, 0.0, qs).astype(BF16)
            kt[di, g] = kst[src, :].T.astype(BF16)
            va[di, dst, 0:LANES] = vst[src, :].astype(BF16)
            va[di, dst, LANES:2 * LANES] = ones

    def attend(di, g, has_prev):
        r0 = pl.multiple_of(g * KEYS_BACK, KEYS_BACK)
        rows = pl.ds(r0, KEYS_BACK)
        if has_prev:
            keys_t = jnp.concatenate([kt[di, g - 1], kt[di, g]], axis=1)
            vals = va[di, pl.ds(r0 - KEYS_BACK, 2 * KEYS_BACK), :]
        else:
            keys_t = kt[di, g]
            vals = va[di, rows, :]
        res = []
        for hh, qsrc in enumerate((q0, q1)):
            s = _dot(qsrc[di, rows, :], keys_t)
            if has_prev:
                s = s + bias_ref[di, hh]
            else:
                s = s + bias_ref[di, hh, :, KEYS_BACK:2 * KEYS_BACK]
            m = jnp.max(s, axis=-1, keepdims=True)
            p = jnp.exp2(s - m).astype(BF16)
            res.append((m, _dot(p, vals)))
        (m0, pv0), (m1, pv1) = res
        ab[di, rows, :] = jnp.where(first_head, pv0[:, :LANES], pv1[:, :LANES])
        lb[di, rows, :] = jnp.where(first_head, pv0[:, LANES:], pv1[:, LANES:])
        mb[di, rows, :] = jnp.where(first_head, m0, m1)

    for di, d in enumerate(DILATIONS):
        blocks_per_residue = nblk // d
        if blocks_per_residue == 1:
            def body(g, c, di=di):
                attend(di, g, False)
                return c
            lax.fori_loop(0, nblk, body, 0)
        else:
            def body(r, c, di=di, bpr=blocks_per_residue):
                g0 = r * bpr
                attend(di, g0, False)

                def inner(j, c2):
                    attend(di, g0 + j, True)
                    return c2
                lax.fori_loop(1, bpr, inner, 0)
                return c
            lax.fori_loop(0, d, body, 0)

    for di, d in enumerate(DILATIONS):
        if d == 1:
            continue
        for g in range(nblk):
            dst = src_rows(d, g)
            src = pl.ds(g * KEYS_BACK, KEYS_BACK)
            um[di - 1, dst, :] = mb[di, src, :]
            ul[di - 1, dst, :] = lb[di, src, :]
            ua[di - 1, dst, :] = ab[di, src, :]

    tile = 2 * KEYS_BACK
    for t in range(seq // tile):
        rows = pl.ds(t * tile, tile)
        ms = [mb[0, rows, :], um[0, rows, :], um[1, rows, :]]
        ls = [lb[0, rows, :], ul[0, rows, :], ul[1, rows, :]]
        accs = [ab[0, rows, :], ua[0, rows, :], ua[1, rows, :]]
        mx = jnp.maximum(jnp.maximum(ms[0], ms[1]), ms[2])
        num = jnp.zeros((tile, LANES), F32)
        den = jnp.zeros((tile, LANES), F32)
        for m, l, a in zip(ms, ls, accs):
            e = jnp.exp2(m - mx)
            num = num + e * a
            den = den + e * l
        o_ref[rows, :] = (num / den).astype(o_ref.dtype)


def _attn_call(q, k, v, bias, batch, seq):
    nd = len(DILATIONS)
    nblk = seq // KEYS_BACK
    n_pairs = B_HEADS // HEADS_PER_STEP
    io_spec = pl.BlockSpec((seq, LANES), lambda hp, b: (b, hp))
    return pl.pallas_call(
        _attn_kernel,
        grid=(n_pairs, batch),
        in_specs=[
            io_spec, io_spec, io_spec,
            pl.BlockSpec((nd, HEADS_PER_STEP, KEYS_BACK, 2 * KEYS_BACK),
                         lambda hp, b: (0, hp, 0, 0)),
        ],
        out_specs=io_spec,
        out_shape=jax.ShapeDtypeStruct((batch * seq, B_WIDTH), BF16),
        scratch_shapes=[
            pltpu.VMEM((seq, LANES), F32),
            pltpu.VMEM((seq, LANES), F32),
            pltpu.VMEM((seq, LANES), F32),
            pltpu.VMEM((nd, seq, LANES), BF16),
            pltpu.VMEM((nd, seq, LANES), BF16),
            pltpu.VMEM((nd, nblk, LANES, KEYS_BACK), BF16),
            pltpu.VMEM((nd, seq, 2 * LANES), BF16),
            pltpu.VMEM((nd, seq, LANES), F32),
            pltpu.VMEM((nd, seq, LANES), F32),
            pltpu.VMEM((nd, seq, LANES), F32),
            pltpu.VMEM((nd - 1, seq, LANES), F32),
            pltpu.VMEM((nd - 1, seq, LANES), F32),
            pltpu.VMEM((nd - 1, seq, LANES), F32),
        ],
        compiler_params=pltpu.CompilerParams(
            dimension_semantics=("arbitrary", "arbitrary"), vmem_limit_bytes=VMEM_LIMIT),
        name="dilated_attn",
    )(q, k, v, bias)


def _t5_bucket_np(dist):
    max_exact = NUM_BUCKETS // 2
    d = np.maximum(dist, 1).astype(np.float64)
    large = max_exact + (np.log(d / max_exact) / math.log(MAX_DISTANCE / max_exact)
                         * (NUM_BUCKETS - max_exact))
    large = np.minimum(large.astype(np.int32), NUM_BUCKETS - 1)
    return np.where(dist < max_exact, dist, large)


def _bias_tables(rel_bias):
    i = np.arange(KEYS_BACK)[:, None]
    j = np.arange(2 * KEYS_BACK)[None, :]
    rel = KEYS_BACK + i - j
    band = (rel >= 0) & (rel <= KEYS_BACK)
    tables = []
    for d in DILATIONS:
        bucket = _t5_bucket_np(np.maximum(rel, 0) * d)
        bias = rel_bias.astype(F32)[bucket] * LOG2E
        bias = jnp.where(band[:, :, None], bias, NEG_INF)
        tables.append(bias.transpose(2, 0, 1))
    return jnp.stack(tables, axis=0)


def _ffn_kernel(x_ref, a_ref, b_ref, wo_ref, gpost_ref, gpre2_ref, wgu_ref, cw_ref, cb_ref,
                wd_ref, gpost2_ref, o_ref, gbuf, carry, *, tiles_per_seq):
    tm = x_ref.shape[0]
    halo = carry.shape[1]
    n_chunks = wd_ref.shape[0] // FF_CHUNK
    first = pl.program_id(0) % tiles_per_seq == 0

    @pl.when(first)
    def _():
        carry[...] = jnp.zeros(carry.shape, F32)

    y1 = _dot(a_ref[...], wo_ref[0:A_WIDTH, :]) + _dot(b_ref[...], wo_ref[A_WIDTH:, :])
    x1 = x_ref[...] + _rms(y1, gpost_ref[...])
    h2 = _rms(x1, gpre2_ref[...]).astype(BF16)

    acc = jnp.zeros((tm, x_ref.shape[1]), F32)
    for c in range(n_chunks):
        cols = slice(c * FF_CHUNK, (c + 1) * FF_CHUNK)
        gu = _dot(h2, wgu_ref[:, 2 * c * FF_CHUNK:2 * (c + 1) * FF_CHUNK])
        g = gu[:, :FF_CHUNK]
        up = gu[:, FF_CHUNK:]
        gbuf[0:halo, :] = carry[c]
        gbuf[halo:halo + tm, :] = g
        carry[c] = g[tm - halo:, :]
        cw = cw_ref[:, cols]
        cv = (cb_ref[:, cols]
              + gbuf[halo - 2:halo - 2 + tm, :] * cw[0:1]
              + gbuf[halo - 1:halo - 1 + tm, :] * cw[1:2]
              + g * cw[2:3])
        act = (jax.nn.gelu(cv) * up).astype(BF16)
        acc = acc + _dot(act, wd_ref[cols, :])
    o_ref[...] = x1 + _rms(acc, gpost2_ref[...])


def _ffn_call(x2, a, b, wo, gpost, gpre2, wgu, cw, cb, wd, gpost2, seq):
    n, d = x2.shape
    tm = TOKENS_PER_STEP
    ff = wd.shape[0]
    halo = 8
    const = lambda i: (0, 0)
    tile = lambda w: pl.BlockSpec((tm, w), lambda i: (i, 0))
    return pl.pallas_call(
        functools.partial(_ffn_kernel, tiles_per_seq=seq // tm),
        grid=(n // tm,),
        in_specs=[
            tile(d), tile(A_WIDTH), tile(B_WIDTH),
            pl.BlockSpec((d, d), const),
            pl.BlockSpec((1, d), const),
            pl.BlockSpec((1, d), const),
            pl.BlockSpec((d, 2 * ff), const),
            pl.BlockSpec((CONV_WIDTH, ff), const),
            pl.BlockSpec((1, ff), const),
            pl.BlockSpec((ff, d), const),
            pl.BlockSpec((1, d), const),
        ],
        out_specs=tile(d),
        out_shape=jax.ShapeDtypeStruct((n, d), F32),
        scratch_shapes=[
            pltpu.VMEM((halo + tm, FF_CHUNK), F32),
            pltpu.VMEM((ff // FF_CHUNK, halo, FF_CHUNK), F32),
        ],
        compiler_params=pltpu.CompilerParams(
            dimension_semantics=("arbitrary",), vmem_limit_bytes=VMEM_LIMIT),
        name="outproj_ffn",
    )(x2, a, b, wo, gpost, gpre2, wgu, cw, cb, wd, gpost2)


def kernel(x, norm_mix_pre, norm_mix_post, norm_ffn_pre, norm_ffn_post, w_in, ln_v_gain, ln_v_bias,
           spatial_w, spatial_b, rel_bias, w_out, w_gate, w_up, conv_w, conv_b, w_down):
    batch, seq, d = x.shape
    depth = w_in.shape[0]
    ff = w_gate.shape[-1]
    n_chunks = ff // FF_CHUNK
    assert seq % TOKENS_PER_STEP == 0 and ff % FF_CHUNK == 0 and d == A_WIDTH + B_WIDTH
    assert seq // KEYS_BACK % max(DILATIONS) == 0

    gmat = jnp.asarray(np.kron(np.eye(A_GROUPS), np.ones((HEAD_DIM, HEAD_DIM))), BF16)
    bias = _bias_tables(rel_bias)
    x2 = x.reshape(batch * seq, d)
    for l in range(depth):
        row = lambda p: p[l].reshape(1, -1).astype(F32)
        sb = jnp.repeat(spatial_b[l].astype(F32).T, HEAD_DIM, axis=1)
        a_out, q, k, v = _proj_call(
            x2, row(norm_mix_pre), w_in[l].astype(BF16), gmat,
            row(ln_v_gain), row(ln_v_bias), spatial_w[l].astype(F32), sb)
        b_out = _attn_call(q, k, v, bias, batch, seq)
        wgu = jnp.concatenate(
            [w_gate[l].astype(BF16).reshape(d, n_chunks, FF_CHUNK),
             w_up[l].astype(BF16).reshape(d, n_chunks, FF_CHUNK)], axis=-1).reshape(d, 2 * ff)
        x2 = _ffn_call(
            x2, a_out, b_out, w_out[l].astype(BF16), row(norm_mix_post), row(norm_ffn_pre),
            wgu, conv_w[l].astype(F32), row(conv_b), w_down[l].astype(BF16),
            row(norm_ffn_post), seq)
    return x2.reshape(batch, seq, d)
```

```python
import functools
import math

import numpy as np
import jax
import jax.numpy as jnp
from jax import lax
from jax.experimental import pallas as pl
from jax.experimental.pallas import tpu as pltpu

F32 = jnp.float32
BF16 = jnp.bfloat16

HEAD_DIM = 64
A_GROUPS = 4
A_WIDTH = A_GROUPS * HEAD_DIM
B_HEADS = 12
B_WIDTH = B_HEADS * HEAD_DIM
CHUNK = 128
DILATIONS = (1, 4, 16)
KEYS_BACK = 128
NUM_BUCKETS = 32
MAX_DISTANCE = 2048
CONV_WIDTH = 3
NORM_EPS = 1e-6
NEG_INF = -1e30
LOG2E = 1.4426950408889634

LANES = 128
HEADS_PER_STEP = LANES // HEAD_DIM
FF_CHUNK = 256
TOKENS_PER_STEP = 512
VMEM_LIMIT = 56 * 1024 * 1024


def _rms(x, g):
    ms = jnp.mean(x * x, axis=-1, keepdims=True)
    return x * lax.rsqrt(ms + NORM_EPS) * g


def _dot(a, b):
    return jnp.dot(a, b, preferred_element_type=F32)


def _dot_nt(a, b):
    return lax.dot_general(a, b, (((1,), (1,)), ((), ())), preferred_element_type=F32)


def _proj_kernel(x_ref, g_ref, w_ref, gmat_ref, lng_ref, lnb_ref, sw_ref, sb_ref,
                 a_ref, q_ref, k_ref, v_ref):
    tm = x_ref.shape[0]
    h = _rms(x_ref[...], g_ref[...]).astype(BF16)

    o = 2 * A_WIDTH
    q_ref[...] = _dot(h, w_ref[:, o:o + B_WIDTH]) * (LOG2E / math.sqrt(HEAD_DIM))
    k_ref[...] = _dot(h, w_ref[:, o + B_WIDTH:o + 2 * B_WIDTH])
    v_ref[...] = _dot(h, w_ref[:, o + 2 * B_WIDTH:o + 3 * B_WIDTH])

    uv = _dot(h, w_ref[:, 0:o])
    u = jax.nn.gelu(uv[:, :A_WIDTH])
    vf = jax.nn.gelu(uv[:, A_WIDTH:])

    gmat = gmat_ref[...]

    def group_sum(t):
        hi = t.astype(BF16)
        lo = (t - hi.astype(F32)).astype(BF16)
        return _dot(hi, gmat) + _dot(lo, gmat)

    mu = group_sum(vf) * (1.0 / HEAD_DIM)
    dl = vf - mu
    var = group_sum(dl * dl) * (1.0 / HEAD_DIM)
    vn = (dl * lax.rsqrt(var + NORM_EPS) * lng_ref[...] + lnb_ref[...]).astype(BF16)

    row = lax.broadcasted_iota(jnp.int32, (CHUNK, CHUNK), 0)
    col = lax.broadcasted_iota(jnp.int32, (CHUNK, CHUNK), 1)
    tril = row >= col
    ws = [jnp.where(tril, sw_ref[g], 0.0).astype(BF16) for g in range(A_GROUPS)]
    grp = lax.broadcasted_iota(jnp.int32, (CHUNK, A_WIDTH), 1) // HEAD_DIM
    sb = sb_ref[...]
    for c in range(tm // CHUNK):
        rows = slice(c * CHUNK, (c + 1) * CHUNK)
        vc = vn[rows]
        z = _dot(ws[0], vc)
        for g in range(1, A_GROUPS):
            z = jnp.where(grp == g, _dot(ws[g], vc), z)
        a_ref[rows, :] = (u[rows] * (z + sb)).astype(BF16)


def _proj_call(x2, g_pre, w_in, gmat, lng, lnb, sw, sb):
    n, d = x2.shape
    tm = TOKENS_PER_STEP
    cols = w_in.shape[1]
    const = lambda i: (0, 0)
    return pl.pallas_call(
        _proj_kernel,
        grid=(n // tm,),
        in_specs=[
            pl.BlockSpec((tm, d), lambda i: (i, 0)),
            pl.BlockSpec((1, d), const),
            pl.BlockSpec((d, cols), const),
            pl.BlockSpec((A_WIDTH, A_WIDTH), const),
            pl.BlockSpec((1, A_WIDTH), const),
            pl.BlockSpec((1, A_WIDTH), const),
            pl.BlockSpec((A_GROUPS, CHUNK, CHUNK), lambda i: (0, 0, 0)),
            pl.BlockSpec((CHUNK, A_WIDTH), const),
        ],
        out_specs=[
            pl.BlockSpec((tm, A_WIDTH), lambda i: (i, 0)),
            pl.BlockSpec((tm, B_WIDTH), lambda i: (i, 0)),
            pl.BlockSpec((tm, B_WIDTH), lambda i: (i, 0)),
            pl.BlockSpec((tm, B_WIDTH), lambda i: (i, 0)),
        ],
        out_shape=[
            jax.ShapeDtypeStruct((n, A_WIDTH), BF16),
            jax.ShapeDtypeStruct((n, B_WIDTH), F32),
            jax.ShapeDtypeStruct((n, B_WIDTH), F32),
            jax.ShapeDtypeStruct((n, B_WIDTH), F32),
        ],
        compiler_params=pltpu.CompilerParams(
            dimension_semantics=("arbitrary",), vmem_limit_bytes=VMEM_LIMIT),
        name="proj_gmlp",
    )(x2, g_pre, w_in, gmat, lng, lnb, sw, sb)


def _t5_bucket_np(dist):
    max_exact = NUM_BUCKETS // 2
    d = np.maximum(dist, 1).astype(np.float64)
    large = max_exact + (np.log(d / max_exact) / math.log(MAX_DISTANCE / max_exact)
                         * (NUM_BUCKETS - max_exact))
    large = np.minimum(large.astype(np.int32), NUM_BUCKETS - 1)
    return np.where(dist < max_exact, dist, large)


def _bucket_starts(d):
    b = _t5_bucket_np(np.arange(KEYS_BACK + 1) * d)
    return [(r, int(b[r])) for r in range(KEYS_BACK + 1) if r == 0 or b[r] != b[r - 1]]


def _attn_kernel(rb_ref, q_ref, k_ref, v_ref, o_ref,
                 qs, kp, va, bt, stg, mb, lb, ab):
    seq = q_ref.shape[0]
    nblk = seq // KEYS_BACK
    padded = seq + KEYS_BACK
    assert DILATIONS == (1, 4, 16) and nblk == DILATIONS[-1]
    lane = lax.broadcasted_iota(jnp.int32, (KEYS_BACK, LANES), 1)
    first_head = lane < HEAD_DIM

    @pl.when(pl.program_id(1) == 0)
    def _():
        i = lax.broadcasted_iota(jnp.int32, (KEYS_BACK, 2 * KEYS_BACK), 0)
        j = lax.broadcasted_iota(jnp.int32, (KEYS_BACK, 2 * KEYS_BACK), 1)
        rel = KEYS_BACK + i - j
        band = (rel >= 0) & (rel <= KEYS_BACK)
        band_cur = band & (j >= KEYS_BACK)
        for di, d in enumerate(DILATIONS):
            for hh in range(HEADS_PER_STEP):
                head = pl.program_id(0) * HEADS_PER_STEP + hh
                val = jnp.zeros(rel.shape, F32)
                for start, bucket in _bucket_starts(d):
                    val = jnp.where(rel >= start, rb_ref[bucket, head], val)
                val = val * LOG2E
                rows = pl.ds(hh * KEYS_BACK, KEYS_BACK)
                bt[2 * di, rows, :] = jnp.where(band, val, NEG_INF)
                bt[2 * di + 1, rows, :] = jnp.where(band_cur, val, NEG_INF)

    quarter = seq // 4

    def d4_rows(c):
        return pl.ds(c // 4 + (c % 4) * KEYS_BACK * 4, KEYS_BACK, stride=4)

    def d16_rows_in_d4_order(r):
        return pl.ds((r % 4) * quarter + r // 4, KEYS_BACK, stride=4)

    def put_block(j, key_row, qv, kv, vv):
        qs[j, 0:KEYS_BACK, :] = jnp.where(first_head, qv, 0.0).astype(BF16)
        qs[j, KEYS_BACK:, :] = jnp.where(first_head, 0.0, qv).astype(BF16)
        kp[pl.ds(key_row, KEYS_BACK), :] = kv.astype(BF16)
        va[pl.ds(key_row, KEYS_BACK), 0:LANES] = vv.astype(BF16)

    @pl.when((pl.program_id(0) == 0) & (pl.program_id(1) == 0))
    def _():
        for di in range(len(DILATIONS)):
            kp[pl.ds(di * padded, KEYS_BACK), :] = jnp.zeros((KEYS_BACK, LANES), BF16)
            va[pl.ds(di * padded, KEYS_BACK), :] = jnp.zeros((KEYS_BACK, 2 * LANES), BF16)
            va[pl.ds(di * padded + KEYS_BACK, seq), LANES:] = jnp.ones((seq, LANES), BF16)

    for g in range(nblk):
        rows = pl.ds(g * KEYS_BACK, KEYS_BACK)
        put_block(g, (g + 1) * KEYS_BACK, q_ref[rows, :], k_ref[rows, :], v_ref[rows, :])
    for g in range(nblk):
        src = d4_rows(g)
        rows = pl.ds(g * KEYS_BACK, KEYS_BACK)
        qv, kv, vv = q_ref[src, :], k_ref[src, :], v_ref[src, :]
        stg[0, rows, :] = qv
        stg[1, rows, :] = kv
        stg[2, rows, :] = vv
        put_block(nblk + g, padded + (g + 1) * KEYS_BACK, qv, kv, vv)
    for g in range(nblk):
        src = d16_rows_in_d4_order(g)
        put_block(2 * nblk + g, 2 * padded + (g + 1) * KEYS_BACK,
                  stg[0, src, :], stg[1, src, :], stg[2, src, :])

    def out_rows(j):
        if j // nblk == 2:
            r = j % nblk
            return pl.ds(2 * seq + (r % 4) * quarter + r // 4, KEYS_BACK, stride=4)
        return pl.ds(j * KEYS_BACK, KEYS_BACK)

    def key_rows(j):
        return pl.ds((j // nblk) * padded + (j % nblk) * KEYS_BACK, 2 * KEYS_BACK)

    def qk(j):
        return _dot_nt(qs[j], kp[key_rows(j), :])

    def softmax(j, s):
        di, g = j // nblk, j % nblk
        no_prev = g % (nblk // DILATIONS[di]) == 0
        s = s + bt[2 * di + int(no_prev)]
        m = jnp.max(s, axis=-1, keepdims=True)
        mb[out_rows(j), :] = jnp.where(first_head, m[:KEYS_BACK], m[KEYS_BACK:])
        return jnp.exp2(s - m).astype(BF16)

    def pv(j, p):
        r = _dot(p, va[key_rows(j), :])
        ab[out_rows(j), :] = jnp.where(first_head, r[:KEYS_BACK, :LANES], r[KEYS_BACK:, :LANES])
        lb[out_rows(j), :] = jnp.where(first_head, r[:KEYS_BACK, LANES:], r[KEYS_BACK:, LANES:])

    n_blocks = len(DILATIONS) * nblk
    for j in range(n_blocks):
        pv(j, softmax(j, qk(j)))

    for c in range(nblk):
        tok = d4_rows(c)
        views = (tok, pl.ds(seq + c * KEYS_BACK, KEYS_BACK), pl.ds(2 * seq + c * KEYS_BACK, KEYS_BACK))
        ms = [mb[v, :] for v in views]
        mx = jnp.maximum(jnp.maximum(ms[0], ms[1]), ms[2])
        num = jnp.zeros((KEYS_BACK, LANES), F32)
        den = jnp.zeros((KEYS_BACK, LANES), F32)
        for m, v in zip(ms, views):
            e = jnp.exp2(m - mx)
            num = num + e * ab[v, :]
            den = den + e * lb[v, :]
        o_ref[tok, :] = num / den


def _attn_call(rel_bias, q, k, v, batch, seq):
    nd = len(DILATIONS)
    nblk = seq // KEYS_BACK
    n_pairs = B_HEADS // HEADS_PER_STEP
    io_spec = pl.BlockSpec((seq, LANES), lambda hp, b: (b, hp))
    return pl.pallas_call(
        _attn_kernel,
        grid=(n_pairs, batch),
        in_specs=[
            pl.BlockSpec(memory_space=pltpu.SMEM),
            io_spec, io_spec, io_spec,
        ],
        out_specs=io_spec,
        out_shape=jax.ShapeDtypeStruct((batch * seq, B_WIDTH), F32),
        scratch_shapes=[
            pltpu.VMEM((nd * nblk, 2 * KEYS_BACK, LANES), BF16),
            pltpu.VMEM((nd * (seq + KEYS_BACK), LANES), BF16),
            pltpu.VMEM((nd * (seq + KEYS_BACK), 2 * LANES), BF16),
            pltpu.VMEM((2 * nd, 2 * KEYS_BACK, 2 * KEYS_BACK), F32),
            pltpu.VMEM((3, seq, LANES), F32),
            pltpu.VMEM((nd * seq, LANES), F32),
            pltpu.VMEM((nd * seq, LANES), F32),
            pltpu.VMEM((nd * seq, LANES), F32),
        ],
        compiler_params=pltpu.CompilerParams(
            dimension_semantics=("arbitrary", "arbitrary"), vmem_limit_bytes=VMEM_LIMIT),
        name="dilated_attn",
    )(rel_bias, q, k, v)


def _ffn_kernel(x_ref, a_ref, b_ref, wo_ref, gpost_ref, gpre2_ref, wgu_ref, cw_ref, cb_ref,
                wd_ref, gpost2_ref, o_ref, gbuf, carry, *, tiles_per_seq):
    tm = x_ref.shape[0]
    halo = carry.shape[1]
    n_chunks = wd_ref.shape[0] // FF_CHUNK
    first = pl.program_id(0) % tiles_per_seq == 0

    @pl.when(first)
    def _():
        carry[...] = jnp.zeros(carry.shape, F32)

    y1 = (_dot(a_ref[...], wo_ref[0:A_WIDTH, :])
          + _dot(b_ref[...].astype(BF16), wo_ref[A_WIDTH:, :]))
    x1 = x_ref[...] + _rms(y1, gpost_ref[...])
    h2 = _rms(x1, gpre2_ref[...]).astype(BF16)

    acc = jnp.zeros((tm, x_ref.shape[1]), F32)
    for c in range(n_chunks):
        cols = slice(c * FF_CHUNK, (c + 1) * FF_CHUNK)
        gu = _dot(h2, wgu_ref[:, 2 * c * FF_CHUNK:2 * (c + 1) * FF_CHUNK])
        g = gu[:, :FF_CHUNK]
        up = gu[:, FF_CHUNK:]
        gbuf[0:halo, :] = carry[c]
        gbuf[halo:halo + tm, :] = g
        carry[c] = g[tm - halo:, :]
        cw = cw_ref[:, cols]
        cv = (cb_ref[:, cols]
              + gbuf[halo - 2:halo - 2 + tm, :] * cw[0:1]
              + gbuf[halo - 1:halo - 1 + tm, :] * cw[1:2]
              + g * cw[2:3])
        act = (jax.nn.gelu(cv) * up).astype(BF16)
        acc = acc + _dot(act, wd_ref[cols, :])
    o_ref[...] = x1 + _rms(acc, gpost2_ref[...])


def _ffn_call(x2, a, b, wo, gpost, gpre2, wgu, cw, cb, wd, gpost2, seq):
    n, d = x2.shape
    tm = TOKENS_PER_STEP
    ff = wd.shape[0]
    halo = 8
    const = lambda i: (0, 0)
    tile = lambda w: pl.BlockSpec((tm, w), lambda i: (i, 0))
    return pl.pallas_call(
        functools.partial(_ffn_kernel, tiles_per_seq=seq // tm),
        grid=(n // tm,),
        in_specs=[
            tile(d), tile(A_WIDTH), tile(B_WIDTH),
            pl.BlockSpec((d, d), const),
            pl.BlockSpec((1, d), const),
            pl.BlockSpec((1, d), const),
            pl.BlockSpec((d, 2 * ff), const),
            pl.BlockSpec((CONV_WIDTH, ff), const),
            pl.BlockSpec((1, ff), const),
            pl.BlockSpec((ff, d), const),
            pl.BlockSpec((1, d), const),
        ],
        out_specs=tile(d),
        out_shape=jax.ShapeDtypeStruct((n, d), F32),
        scratch_shapes=[
            pltpu.VMEM((halo + tm, FF_CHUNK), F32),
            pltpu.VMEM((ff // FF_CHUNK, halo, FF_CHUNK), F32),
        ],
        compiler_params=pltpu.CompilerParams(
            dimension_semantics=("arbitrary",), vmem_limit_bytes=VMEM_LIMIT),
        name="outproj_ffn",
    )(x2, a, b, wo, gpost, gpre2, wgu, cw, cb, wd, gpost2)


def kernel(x, norm_mix_pre, norm_mix_post, norm_ffn_pre, norm_ffn_post, w_in, ln_v_gain, ln_v_bias,
           spatial_w, spatial_b, rel_bias, w_out, w_gate, w_up, conv_w, conv_b, w_down):
    batch, seq, d = x.shape
    depth = w_in.shape[0]
    ff = w_gate.shape[-1]
    n_chunks = ff // FF_CHUNK
    assert seq % TOKENS_PER_STEP == 0 and ff % FF_CHUNK == 0 and d == A_WIDTH + B_WIDTH
    assert seq // KEYS_BACK == max(DILATIONS)

    gmat = jnp.asarray(np.kron(np.eye(A_GROUPS), np.ones((HEAD_DIM, HEAD_DIM))), BF16)
    x2 = x.reshape(batch * seq, d)
    for l in range(depth):
        row = lambda p: p[l].reshape(1, -1).astype(F32)
        sb = jnp.repeat(spatial_b[l].astype(F32).T, HEAD_DIM, axis=1)
        a_out, q, k, v = _proj_call(
            x2, row(norm_mix_pre), w_in[l].astype(BF16), gmat,
            row(ln_v_gain), row(ln_v_bias), spatial_w[l].astype(F32), sb)
        b_out = _attn_call(rel_bias.astype(F32), q, k, v, batch, seq)
        wgu = jnp.concatenate(
            [w_gate[l].astype(BF16).reshape(d, n_chunks, FF_CHUNK),
             w_up[l].astype(BF16).reshape(d, n_chunks, FF_CHUNK)], axis=-1).reshape(d, 2 * ff)
        x2 = _ffn_call(
            x2, a_out, b_out, w_out[l].astype(BF16), row(norm_mix_post), row(norm_ffn_pre),
            wgu, conv_w[l].astype(F32), row(conv_b), w_down[l].astype(BF16),
            row(norm_ffn_post), seq)
    return x2.reshape(batch, seq, d)
```

```python
import functools
import math

import numpy as np
import jax
import jax.numpy as jnp
from jax import lax
from jax.experimental import pallas as pl
from jax.experimental.pallas import tpu as pltpu

F32 = jnp.float32
BF16 = jnp.bfloat16

HEAD_DIM = 64
A_GROUPS = 4
A_WIDTH = A_GROUPS * HEAD_DIM
B_HEADS = 12
B_WIDTH = B_HEADS * HEAD_DIM
CHUNK = 128
DILATIONS = (1, 4, 16)
KEYS_BACK = 128
NUM_BUCKETS = 32
MAX_DISTANCE = 2048
CONV_WIDTH = 3
NORM_EPS = 1e-6
NEG_INF = -1e30
LOG2E = 1.4426950408889634

LANES = 128
HEADS_PER_STEP = LANES // HEAD_DIM
FF_CHUNK = 256
TOKENS_PER_STEP = 512
VMEM_LIMIT = 56 * 1024 * 1024


def _rms(x, g):
    ms = jnp.mean(x * x, axis=-1, keepdims=True)
    return x * lax.rsqrt(ms + NORM_EPS) * g


def _dot(a, b):
    return jnp.dot(a, b, preferred_element_type=F32)


def _dot_nt(a, b):
    return lax.dot_general(a, b, (((1,), (1,)), ((), ())), preferred_element_type=F32)


def _proj_kernel(x_ref, g_ref, w_ref, gmat_ref, lng_ref, lnb_ref, sw_ref, sb_ref,
                 a_ref, q_ref, k_ref, v_ref):
    tm = x_ref.shape[0]
    h = _rms(x_ref[...], g_ref[...]).astype(BF16)

    o = 2 * A_WIDTH
    q_ref[...] = _dot(h, w_ref[:, o:o + B_WIDTH]) * (LOG2E / math.sqrt(HEAD_DIM))
    k_ref[...] = _dot(h, w_ref[:, o + B_WIDTH:o + 2 * B_WIDTH])
    v_ref[...] = _dot(h, w_ref[:, o + 2 * B_WIDTH:o + 3 * B_WIDTH])

    uv = _dot(h, w_ref[:, 0:o])
    u = jax.nn.gelu(uv[:, :A_WIDTH])
    vf = jax.nn.gelu(uv[:, A_WIDTH:])

    gmat = gmat_ref[...]

    def group_sum(t):
        hi = t.astype(BF16)
        lo = (t - hi.astype(F32)).astype(BF16)
        return _dot(hi, gmat) + _dot(lo, gmat)

    mu = group_sum(vf) * (1.0 / HEAD_DIM)
    dl = vf - mu
    var = group_sum(dl * dl) * (1.0 / HEAD_DIM)
    vn = (dl * lax.rsqrt(var + NORM_EPS) * lng_ref[...] + lnb_ref[...]).astype(BF16)

    row = lax.broadcasted_iota(jnp.int32, (CHUNK, CHUNK), 0)
    col = lax.broadcasted_iota(jnp.int32, (CHUNK, CHUNK), 1)
    tril = row >= col
    ws = [jnp.where(tril, sw_ref[g], 0.0).astype(BF16) for g in range(A_GROUPS)]
    grp = lax.broadcasted_iota(jnp.int32, (CHUNK, A_WIDTH), 1) // HEAD_DIM
    sb = sb_ref[...]
    for c in range(tm // CHUNK):
        rows = slice(c * CHUNK, (c + 1) * CHUNK)
        vc = vn[rows]
        z = _dot(ws[0], vc)
        for g in range(1, A_GROUPS):
            z = jnp.where(grp == g, _dot(ws[g], vc), z)
        a_ref[rows, :] = (u[rows] * (z + sb)).astype(BF16)


def _proj_call(x2, g_pre, w_in, gmat, lng, lnb, sw, sb):
    n, d = x2.shape
    tm = TOKENS_PER_STEP
    cols = w_in.shape[1]
    const = lambda i: (0, 0)
    return pl.pallas_call(
        _proj_kernel,
        grid=(n // tm,),
        in_specs=[
            pl.BlockSpec((tm, d), lambda i: (i, 0)),
            pl.BlockSpec((1, d), const),
            pl.BlockSpec((d, cols), const),
            pl.BlockSpec((A_WIDTH, A_WIDTH), const),
            pl.BlockSpec((1, A_WIDTH), const),
            pl.BlockSpec((1, A_WIDTH), const),
            pl.BlockSpec((A_GROUPS, CHUNK, CHUNK), lambda i: (0, 0, 0)),
            pl.BlockSpec((CHUNK, A_WIDTH), const),
        ],
        out_specs=[
            pl.BlockSpec((tm, A_WIDTH), lambda i: (i, 0)),
            pl.BlockSpec((tm, B_WIDTH), lambda i: (i, 0)),
            pl.BlockSpec((tm, B_WIDTH), lambda i: (i, 0)),
            pl.BlockSpec((tm, B_WIDTH), lambda i: (i, 0)),
        ],
        out_shape=[
            jax.ShapeDtypeStruct((n, A_WIDTH), BF16),
            jax.ShapeDtypeStruct((n, B_WIDTH), F32),
            jax.ShapeDtypeStruct((n, B_WIDTH), F32),
            jax.ShapeDtypeStruct((n, B_WIDTH), F32),
        ],
        compiler_params=pltpu.CompilerParams(
            dimension_semantics=("arbitrary",), vmem_limit_bytes=VMEM_LIMIT),
        name="proj_gmlp",
    )(x2, g_pre, w_in, gmat, lng, lnb, sw, sb)


def _t5_bucket_np(dist):
    max_exact = NUM_BUCKETS // 2
    d = np.maximum(dist, 1).astype(np.float64)
    large = max_exact + (np.log(d / max_exact) / math.log(MAX_DISTANCE / max_exact)
                         * (NUM_BUCKETS - max_exact))
    large = np.minimum(large.astype(np.int32), NUM_BUCKETS - 1)
    return np.where(dist < max_exact, dist, large)


def _bucket_starts(d):
    b = _t5_bucket_np(np.arange(KEYS_BACK + 1) * d)
    return [(r, int(b[r])) for r in range(KEYS_BACK + 1) if r == 0 or b[r] != b[r - 1]]


def _attn_kernel(rb_ref, q_ref, k_ref, v_ref, o_ref,
                 qs, kp, va, bt, stg, mb, lb, ab):
    seq = q_ref.shape[0]
    nblk = seq // KEYS_BACK
    padded = seq + KEYS_BACK
    assert DILATIONS == (1, 4, 16) and nblk == DILATIONS[-1]
    lane = lax.broadcasted_iota(jnp.int32, (KEYS_BACK, LANES), 1)
    first_head = lane < HEAD_DIM

    @pl.when(pl.program_id(1) == 0)
    def _():
        i = lax.broadcasted_iota(jnp.int32, (KEYS_BACK, 2 * KEYS_BACK), 0)
        j = lax.broadcasted_iota(jnp.int32, (KEYS_BACK, 2 * KEYS_BACK), 1)
        rel = KEYS_BACK + i - j
        band = (rel >= 0) & (rel <= KEYS_BACK)
        band_cur = band & (j >= KEYS_BACK)
        for di, d in enumerate(DILATIONS):
            for hh in range(HEADS_PER_STEP):
                head = pl.program_id(0) * HEADS_PER_STEP + hh
                val = jnp.zeros(rel.shape, F32)
                for start, bucket in _bucket_starts(d):
                    val = jnp.where(rel >= start, rb_ref[bucket, head], val)
                val = val * LOG2E
                rows = pl.ds(hh * KEYS_BACK, KEYS_BACK)
                bt[2 * di, rows, :] = jnp.where(band, val, NEG_INF)
                bt[2 * di + 1, rows, :] = jnp.where(band_cur, val, NEG_INF)

    quarter = seq // 4

    def d4_rows(c):
        return pl.ds(c // 4 + (c % 4) * KEYS_BACK * 4, KEYS_BACK, stride=4)

    def d16_rows_in_d4_order(r):
        return pl.ds((r % 4) * quarter + r // 4, KEYS_BACK, stride=4)

    def put_block(j, key_row, qv, kv, vv):
        qs[j, 0:KEYS_BACK, :] = jnp.where(first_head, qv, 0.0).astype(BF16)
        qs[j, KEYS_BACK:, :] = jnp.where(first_head, 0.0, qv).astype(BF16)
        kp[pl.ds(key_row, KEYS_BACK), :] = kv.astype(BF16)
        va[pl.ds(key_row, KEYS_BACK), 0:LANES] = vv.astype(BF16)

    @pl.when((pl.program_id(0) == 0) & (pl.program_id(1) == 0))
    def _():
        for di in range(len(DILATIONS)):
            kp[pl.ds(di * padded, KEYS_BACK), :] = jnp.zeros((KEYS_BACK, LANES), BF16)
            va[pl.ds(di * padded, KEYS_BACK), :] = jnp.zeros((KEYS_BACK, 2 * LANES), BF16)
            va[pl.ds(di * padded + KEYS_BACK, seq), LANES:] = jnp.ones((seq, LANES), BF16)

    for g in range(nblk):
        rows = pl.ds(g * KEYS_BACK, KEYS_BACK)
        put_block(g, (g + 1) * KEYS_BACK, q_ref[rows, :], k_ref[rows, :], v_ref[rows, :])
    for g in range(nblk):
        src = d4_rows(g)
        rows = pl.ds(g * KEYS_BACK, KEYS_BACK)
        qv, kv, vv = q_ref[src, :], k_ref[src, :], v_ref[src, :]
        stg[0, rows, :] = qv
        stg[1, rows, :] = kv
        stg[2, rows, :] = vv
        put_block(nblk + g, padded + (g + 1) * KEYS_BACK, qv, kv, vv)
    for g in range(nblk):
        src = d16_rows_in_d4_order(g)
        put_block(2 * nblk + g, 2 * padded + (g + 1) * KEYS_BACK,
                  stg[0, src, :], stg[1, src, :], stg[2, src, :])

    def out_rows(j):
        if j // nblk == 2:
            r = j % nblk
            return pl.ds(2 * seq + (r % 4) * quarter + r // 4, KEYS_BACK, stride=4)
        return pl.ds(j * KEYS_BACK, KEYS_BACK)

    def key_rows(j):
        return pl.ds((j // nblk) * padded + (j % nblk) * KEYS_BACK, 2 * KEYS_BACK)

    def qk(j):
        return _dot_nt(qs[j], kp[key_rows(j), :])

    def softmax(j, s):
        di, g = j // nblk, j % nblk
        no_prev = g % (nblk // DILATIONS[di]) == 0
        s = s + bt[2 * di + int(no_prev)]
        m = jnp.max(s, axis=-1, keepdims=True)
        mb[out_rows(j), :] = jnp.where(first_head, m[:KEYS_BACK], m[KEYS_BACK:])
        return jnp.exp2(s - m).astype(BF16)

    def pv(j, p):
        r = _dot(p, va[key_rows(j), :])
        ab[out_rows(j), :] = jnp.where(first_head, r[:KEYS_BACK, :LANES], r[KEYS_BACK:, :LANES])
        lb[out_rows(j), :] = jnp.where(first_head, r[:KEYS_BACK, LANES:], r[KEYS_BACK:, LANES:])

    n_blocks = len(DILATIONS) * nblk
    for j in range(n_blocks):
        pv(j, softmax(j, qk(j)))

    for c in range(nblk):
        tok = d4_rows(c)
        views = (tok, pl.ds(seq + c * KEYS_BACK, KEYS_BACK), pl.ds(2 * seq + c * KEYS_BACK, KEYS_BACK))
        ms = [mb[v, :] for v in views]
        mx = jnp.maximum(jnp.maximum(ms[0], ms[1]), ms[2])
        num = jnp.zeros((KEYS_BACK, LANES), F32)
        den = jnp.zeros((KEYS_BACK, LANES), F32)
        for m, v in zip(ms, views):
            e = jnp.exp2(m - mx)
            num = num + e * ab[v, :]
            den = den + e * lb[v, :]
        o_ref[tok, :] = num / den


def _attn_call(rel_bias, q, k, v, batch, seq):
    nd = len(DILATIONS)
    nblk = seq // KEYS_BACK
    n_pairs = B_HEADS // HEADS_PER_STEP
    io_spec = pl.BlockSpec((seq, LANES), lambda hp, b: (b, hp))
    return pl.pallas_call(
        _attn_kernel,
        grid=(n_pairs, batch),
        in_specs=[
            pl.BlockSpec(memory_space=pltpu.SMEM),
            io_spec, io_spec, io_spec,
        ],
        out_specs=io_spec,
        out_shape=jax.ShapeDtypeStruct((batch * seq, B_WIDTH), F32),
        scratch_shapes=[
            pltpu.VMEM((nd * nblk, 2 * KEYS_BACK, LANES), BF16),
            pltpu.VMEM((nd * (seq + KEYS_BACK), LANES), BF16),
            pltpu.VMEM((nd * (seq + KEYS_BACK), 2 * LANES), BF16),
            pltpu.VMEM((2 * nd, 2 * KEYS_BACK, 2 * KEYS_BACK), F32),
            pltpu.VMEM((3, seq, LANES), F32),
            pltpu.VMEM((nd * seq, LANES), F32),
            pltpu.VMEM((nd * seq, LANES), F32),
            pltpu.VMEM((nd * seq, LANES), F32),
        ],
        compiler_params=pltpu.CompilerParams(
            dimension_semantics=("arbitrary", "arbitrary"), vmem_limit_bytes=VMEM_LIMIT),
        name="dilated_attn",
    )(rel_bias, q, k, v)


def _ffn_kernel(x_ref, a_ref, b_ref, wo_ref, gpost_ref, gpre2_ref, wg_ref, wu_ref, cw_ref, cb_ref,
                wd_ref, gpost2_ref, o_ref, gbuf, carry, act_buf, *, tiles_per_seq):
    tm = x_ref.shape[0]
    halo = carry.shape[1]
    n_chunks = wd_ref.shape[0] // FF_CHUNK
    first = pl.program_id(0) % tiles_per_seq == 0

    @pl.when(first)
    def _():
        carry[...] = jnp.zeros(carry.shape, F32)

    y1 = (_dot(a_ref[...], wo_ref[0:A_WIDTH, :])
          + _dot(b_ref[...].astype(BF16), wo_ref[A_WIDTH:, :]))
    x1 = x_ref[...] + _rms(y1, gpost_ref[...])
    h2 = _rms(x1, gpre2_ref[...]).astype(BF16)

    def gate_up(c):
        cols = slice(c * FF_CHUNK, (c + 1) * FF_CHUNK)
        return _dot(h2, wg_ref[:, cols]), _dot(h2, wu_ref[:, cols])

    gu = gate_up(0)
    for c in range(n_chunks):
        cols = slice(c * FF_CHUNK, (c + 1) * FF_CHUNK)
        gu_next = gate_up(c + 1) if c + 1 < n_chunks else None
        g, up = gu
        gbuf[0:halo, :] = carry[c]
        gbuf[halo:halo + tm, :] = g
        carry[c] = g[tm - halo:, :]
        cw = cw_ref[:, cols]
        cv = (cb_ref[:, cols]
              + gbuf[halo - 2:halo - 2 + tm, :] * cw[0:1]
              + gbuf[halo - 1:halo - 1 + tm, :] * cw[1:2]
              + g * cw[2:3])
        act_buf[:, cols] = (jax.nn.gelu(cv) * up).astype(BF16)
        gu = gu_next
    o_ref[...] = x1 + _rms(_dot(act_buf[...], wd_ref[...]), gpost2_ref[...])


def _ffn_call(x2, a, b, wo, gpost, gpre2, wg, wu, cw, cb, wd, gpost2, seq):
    n, d = x2.shape
    tm = TOKENS_PER_STEP
    ff = wd.shape[0]
    halo = 8
    const = lambda i: (0, 0)
    tile = lambda w: pl.BlockSpec((tm, w), lambda i: (i, 0))
    return pl.pallas_call(
        functools.partial(_ffn_kernel, tiles_per_seq=seq // tm),
        grid=(n // tm,),
        in_specs=[
            tile(d), tile(A_WIDTH), tile(B_WIDTH),
            pl.BlockSpec((d, d), const),
            pl.BlockSpec((1, d), const),
            pl.BlockSpec((1, d), const),
            pl.BlockSpec((d, ff), const),
            pl.BlockSpec((d, ff), const),
            pl.BlockSpec((CONV_WIDTH, ff), const),
            pl.BlockSpec((1, ff), const),
            pl.BlockSpec((ff, d), const),
            pl.BlockSpec((1, d), const),
        ],
        out_specs=tile(d),
        out_shape=jax.ShapeDtypeStruct((n, d), F32),
        scratch_shapes=[
            pltpu.VMEM((halo + tm, FF_CHUNK), F32),
            pltpu.VMEM((ff // FF_CHUNK, halo, FF_CHUNK), F32),
            pltpu.VMEM((tm, ff), BF16),
        ],
        compiler_params=pltpu.CompilerParams(
            dimension_semantics=("arbitrary",), vmem_limit_bytes=VMEM_LIMIT),
        name="outproj_ffn",
    )(x2, a, b, wo, gpost, gpre2, wg, wu, cw, cb, wd, gpost2)


def kernel(x, norm_mix_pre, norm_mix_post, norm_ffn_pre, norm_ffn_post, w_in, ln_v_gain, ln_v_bias,
           spatial_w, spatial_b, rel_bias, w_out, w_gate, w_up, conv_w, conv_b, w_down):
    batch, seq, d = x.shape
    depth = w_in.shape[0]
    ff = w_gate.shape[-1]
    assert seq % TOKENS_PER_STEP == 0 and ff % FF_CHUNK == 0 and d == A_WIDTH + B_WIDTH
    assert seq // KEYS_BACK == max(DILATIONS)

    gmat = jnp.asarray(np.kron(np.eye(A_GROUPS), np.ones((HEAD_DIM, HEAD_DIM))), BF16)
    x2 = x.reshape(batch * seq, d)
    for l in range(depth):
        row = lambda p: p[l].reshape(1, -1).astype(F32)
        sb = jnp.repeat(spatial_b[l].astype(F32).T, HEAD_DIM, axis=1)
        a_out, q, k, v = _proj_call(
            x2, row(norm_mix_pre), w_in[l].astype(BF16), gmat,
            row(ln_v_gain), row(ln_v_bias), spatial_w[l].astype(F32), sb)
        b_out = _attn_call(rel_bias.astype(F32), q, k, v, batch, seq)
        x2 = _ffn_call(
            x2, a_out, b_out, w_out[l].astype(BF16), row(norm_mix_post), row(norm_ffn_pre),
            w_gate[l].astype(BF16), w_up[l].astype(BF16), conv_w[l].astype(F32), row(conv_b),
            w_down[l].astype(BF16),
            row(norm_ffn_post), seq)
    return x2.reshape(batch, seq, d)
```

```python
import functools
import math

import numpy as np
import jax
import jax.numpy as jnp
from jax import lax
from jax.experimental import pallas as pl
from jax.experimental.pallas import tpu as pltpu

F32 = jnp.float32
BF16 = jnp.bfloat16

HEAD_DIM = 64
A_GROUPS = 4
A_WIDTH = A_GROUPS * HEAD_DIM
B_HEADS = 12
B_WIDTH = B_HEADS * HEAD_DIM
CHUNK = 128
DILATIONS = (1, 4, 16)
KEYS_BACK = 128
NUM_BUCKETS = 32
MAX_DISTANCE = 2048
CONV_WIDTH = 3
NORM_EPS = 1e-6
NEG_INF = -1e30
LOG2E = 1.4426950408889634

LANES = 128
HEADS_PER_STEP = LANES // HEAD_DIM
FF_CHUNK = 256
TOKENS_PER_STEP = 1024
ROWS_PER_PASS = 512
VMEM_LIMIT = 56 * 1024 * 1024


def _rms(x, g):
    ms = jnp.mean(x * x, axis=-1, keepdims=True)
    return x * lax.rsqrt(ms + NORM_EPS) * g


def _dot(a, b):
    return jnp.dot(a, b, preferred_element_type=F32)


def _dot_nt(a, b):
    return lax.dot_general(a, b, (((1,), (1,)), ((), ())), preferred_element_type=F32)


def _proj_kernel(x_ref, g_ref, w_ref, gmat_ref, lng_ref, lnb_ref, sw_ref, sb_ref,
                 a_ref, q_ref, k_ref, v_ref):
    gmat = gmat_ref[...]
    row = lax.broadcasted_iota(jnp.int32, (CHUNK, CHUNK), 0)
    col = lax.broadcasted_iota(jnp.int32, (CHUNK, CHUNK), 1)
    tril = row >= col
    ws = [jnp.where(tril, sw_ref[g], 0.0).astype(BF16) for g in range(A_GROUPS)]
    grp = lax.broadcasted_iota(jnp.int32, (CHUNK, A_WIDTH), 1) // HEAD_DIM
    sb = sb_ref[...]
    o = 2 * A_WIDTH

    def group_sum(t):
        hi = t.astype(BF16)
        lo = (t - hi.astype(F32)).astype(BF16)
        return _dot(hi, gmat) + _dot(lo, gmat)

    for p in range(x_ref.shape[0] // ROWS_PER_PASS):
        rows = pl.ds(p * ROWS_PER_PASS, ROWS_PER_PASS)
        h = _rms(x_ref[rows, :], g_ref[...]).astype(BF16)

        q_ref[rows, :] = _dot(h, w_ref[:, o:o + B_WIDTH]) * (LOG2E / math.sqrt(HEAD_DIM))
        k_ref[rows, :] = _dot(h, w_ref[:, o + B_WIDTH:o + 2 * B_WIDTH])
        v_ref[rows, :] = _dot(h, w_ref[:, o + 2 * B_WIDTH:o + 3 * B_WIDTH])

        uv = _dot(h, w_ref[:, 0:o])
        u = jax.nn.gelu(uv[:, :A_WIDTH])
        vf = jax.nn.gelu(uv[:, A_WIDTH:])
        mu = group_sum(vf) * (1.0 / HEAD_DIM)
        dl = vf - mu
        var = group_sum(dl * dl) * (1.0 / HEAD_DIM)
        vn = (dl * lax.rsqrt(var + NORM_EPS) * lng_ref[...] + lnb_ref[...]).astype(BF16)

        for c in range(ROWS_PER_PASS // CHUNK):
            chunk = slice(c * CHUNK, (c + 1) * CHUNK)
            vc = vn[chunk]
            z = _dot(ws[0], vc)
            for g in range(1, A_GROUPS):
                z = jnp.where(grp == g, _dot(ws[g], vc), z)
            a_ref[pl.ds(p * ROWS_PER_PASS + c * CHUNK, CHUNK), :] = (u[chunk] * (z + sb)).astype(BF16)


def _proj_call(x2, g_pre, w_in, gmat, lng, lnb, sw, sb):
    n, d = x2.shape
    tm = TOKENS_PER_STEP
    cols = w_in.shape[1]
    const = lambda i: (0, 0)
    once = pl.Buffered(1)
    return pl.pallas_call(
        _proj_kernel,
        grid=(n // tm,),
        in_specs=[
            pl.BlockSpec((tm, d), lambda i: (i, 0)),
            pl.BlockSpec((1, d), const, pipeline_mode=once),
            pl.BlockSpec((d, cols), const, pipeline_mode=once),
            pl.BlockSpec((A_WIDTH, A_WIDTH), const, pipeline_mode=once),
            pl.BlockSpec((1, A_WIDTH), const, pipeline_mode=once),
            pl.BlockSpec((1, A_WIDTH), const, pipeline_mode=once),
            pl.BlockSpec((A_GROUPS, CHUNK, CHUNK), lambda i: (0, 0, 0), pipeline_mode=once),
            pl.BlockSpec((CHUNK, A_WIDTH), const, pipeline_mode=once),
        ],
        out_specs=[
            pl.BlockSpec((tm, A_WIDTH), lambda i: (i, 0)),
            pl.BlockSpec((tm, B_WIDTH), lambda i: (i, 0)),
            pl.BlockSpec((tm, B_WIDTH), lambda i: (i, 0)),
            pl.BlockSpec((tm, B_WIDTH), lambda i: (i, 0)),
        ],
        out_shape=[
            jax.ShapeDtypeStruct((n, A_WIDTH), BF16),
            jax.ShapeDtypeStruct((n, B_WIDTH), F32),
            jax.ShapeDtypeStruct((n, B_WIDTH), F32),
            jax.ShapeDtypeStruct((n, B_WIDTH), F32),
        ],
        compiler_params=pltpu.CompilerParams(
            dimension_semantics=("arbitrary",), vmem_limit_bytes=VMEM_LIMIT),
        name="proj_gmlp",
    )(x2, g_pre, w_in, gmat, lng, lnb, sw, sb)


def _t5_bucket_np(dist):
    max_exact = NUM_BUCKETS // 2
    d = np.maximum(dist, 1).astype(np.float64)
    large = max_exact + (np.log(d / max_exact) / math.log(MAX_DISTANCE / max_exact)
                         * (NUM_BUCKETS - max_exact))
    large = np.minimum(large.astype(np.int32), NUM_BUCKETS - 1)
    return np.where(dist < max_exact, dist, large)


def _bucket_starts(d):
    b = _t5_bucket_np(np.arange(KEYS_BACK + 1) * d)
    return [(r, int(b[r])) for r in range(KEYS_BACK + 1) if r == 0 or b[r] != b[r - 1]]


def _attn_kernel(rb_ref, q_ref, k_ref, v_ref, o_ref,
                 qs, kp, va, bt, stg, mb, lb, ab):
    seq = q_ref.shape[0]
    nblk = seq // KEYS_BACK
    assert DILATIONS == (1, 4, 16) and nblk == DILATIONS[-1]
    lane = lax.broadcasted_iota(jnp.int32, (KEYS_BACK, LANES), 1)
    first_head = lane < HEAD_DIM

    @pl.when(pl.program_id(1) == 0)
    def _():
        i = lax.broadcasted_iota(jnp.int32, (KEYS_BACK, 2 * KEYS_BACK), 0)
        j = lax.broadcasted_iota(jnp.int32, (KEYS_BACK, 2 * KEYS_BACK), 1)
        rel = KEYS_BACK + i - j
        band = (rel >= 0) & (rel <= KEYS_BACK)
        band_cur = band & (j >= KEYS_BACK)
        for di, d in enumerate(DILATIONS):
            for hh in range(HEADS_PER_STEP):
                head = pl.program_id(0) * HEADS_PER_STEP + hh
                val = jnp.zeros(rel.shape, F32)
                for start, bucket in _bucket_starts(d):
                    val = jnp.where(rel >= start, rb_ref[bucket, head], val)
                val = val * LOG2E
                rows = pl.ds(hh * KEYS_BACK, KEYS_BACK)
                bt[2 * di, rows, :] = jnp.where(band, val, NEG_INF)
                bt[2 * di + 1, rows, :] = jnp.where(band_cur, val, NEG_INF)

    quarter = seq // 4
    n_blocks = len(DILATIONS) * nblk

    def d4_rows(c):
        return pl.ds(c // 4 + (c % 4) * KEYS_BACK * 4, KEYS_BACK, stride=4)

    def d16_rows_in_d4_order(r):
        return pl.ds((r % 4) * quarter + r // 4, KEYS_BACK, stride=4)

    def block_rows(j):
        return pl.ds(j * KEYS_BACK, KEYS_BACK)

    @pl.when((pl.program_id(0) == 0) & (pl.program_id(1) == 0))
    def _():
        kp[...] = jnp.zeros(kp.shape, BF16)
        rows_total = va.shape[1]
        lane_all = lax.broadcasted_iota(jnp.int32, (rows_total, LANES), 1)
        for hh in range(HEADS_PER_STEP):
            own = (lane_all // HEAD_DIM) == hh
            va[hh, :, 0:LANES] = jnp.zeros((rows_total, LANES), BF16)
            va[hh, :, LANES:] = jnp.where(own, 1.0, 0.0).astype(BF16)

    def regroup(j):
        di, g = j // nblk, j % nblk
        if di == 0:
            qv, kv, vv = (ref[block_rows(g), :] for ref in (q_ref, k_ref, v_ref))
        elif di == 1:
            qv, kv, vv = (ref[d4_rows(g), :] for ref in (q_ref, k_ref, v_ref))
            for n, val in enumerate((qv, kv, vv)):
                stg[n, block_rows(g), :] = val
        else:
            qv, kv, vv = (stg[n, d16_rows_in_d4_order(g), :] for n in range(3))
        qs[j, 0:KEYS_BACK, :] = jnp.where(first_head, qv, 0.0).astype(BF16)
        qs[j, KEYS_BACK:, :] = jnp.where(first_head, 0.0, qv).astype(BF16)
        kp[block_rows(j + 1), :] = kv.astype(BF16)
        va[0, block_rows(j + 1), 0:LANES] = jnp.where(first_head, vv, 0.0).astype(BF16)
        va[1, block_rows(j + 1), 0:LANES] = jnp.where(first_head, 0.0, vv).astype(BF16)

    def out_rows(j):
        if j // nblk == 2:
            r = j % nblk
            return pl.ds(2 * seq + (r % 4) * quarter + r // 4, KEYS_BACK, stride=4)
        return block_rows(j)

    def attend(j):
        di, g = j // nblk, j % nblk
        no_prev = g % (nblk // DILATIONS[di]) == 0
        keys = pl.ds(j * KEYS_BACK, 2 * KEYS_BACK)
        s = _dot_nt(qs[j], kp[keys, :]) + bt[2 * di + int(no_prev)]
        m = jnp.max(s, axis=-1, keepdims=True)
        mb[out_rows(j), :] = jnp.where(first_head, m[:KEYS_BACK], m[KEYS_BACK:])
        p = jnp.exp2(s - m).astype(BF16)
        p = jnp.concatenate([p[:KEYS_BACK], p[KEYS_BACK:]], axis=1)
        vals = jnp.concatenate([va[0, keys, :], va[1, keys, :]], axis=0)
        r = _dot(p, vals)
        ab[out_rows(j), :] = r[:, :LANES]
        lb[out_rows(j), :] = r[:, LANES:]

    def combine(c):
        tok = d4_rows(c)
        views = (tok, block_rows(nblk + c), block_rows(2 * nblk + c))
        ms = [mb[v, :] for v in views]
        mx = jnp.maximum(jnp.maximum(ms[0], ms[1]), ms[2])
        e0, e1, e2 = (jnp.exp2(m - mx) for m in ms)
        v0, v1, v2 = views
        num = e0 * ab[v0, :] + e1 * ab[v1, :] + e2 * ab[v2, :]
        den = e0 * lb[v0, :] + e1 * lb[v1, :] + e2 * lb[v2, :]
        o_ref[tok, :] = num / den

    d4 = [nblk + g for g in range(nblk)]
    d16 = [[2 * nblk + r for r in range(k, nblk, 4)] for k in range(4)]
    regroup_order = [j for k in range(4) for j in d4[4 * k:4 * k + 4] + d16[k]] + list(range(nblk))
    attend_order = [j for grp in d16 for j in grp] + list(range(nblk)) + d4
    ahead = 3
    done = 0
    for j in attend_order:
        want = min(regroup_order.index(j) + 1 + ahead, n_blocks)
        for jj in regroup_order[done:want]:
            regroup(jj)
        done = max(done, want)
        attend(j)
        if j // nblk == 1:
            combine(j % nblk)


def _attn_call(rel_bias, q, k, v, batch, seq):
    nd = len(DILATIONS)
    nblk = seq // KEYS_BACK
    n_pairs = B_HEADS // HEADS_PER_STEP
    io_spec = pl.BlockSpec((seq, LANES), lambda hp, b: (b, hp))
    return pl.pallas_call(
        _attn_kernel,
        grid=(n_pairs, batch),
        in_specs=[
            pl.BlockSpec(memory_space=pltpu.SMEM),
            io_spec, io_spec, io_spec,
        ],
        out_specs=io_spec,
        out_shape=jax.ShapeDtypeStruct((batch * seq, B_WIDTH), F32),
        scratch_shapes=[
            pltpu.VMEM((nd * nblk, 2 * KEYS_BACK, LANES), BF16),
            pltpu.VMEM((nd * seq + KEYS_BACK, LANES), BF16),
            pltpu.VMEM((HEADS_PER_STEP, nd * seq + KEYS_BACK, 2 * LANES), BF16),
            pltpu.VMEM((2 * nd, 2 * KEYS_BACK, 2 * KEYS_BACK), F32),
            pltpu.VMEM((3, seq, LANES), F32),
            pltpu.VMEM((nd * seq, LANES), F32),
            pltpu.VMEM((nd * seq, LANES), F32),
            pltpu.VMEM((nd * seq, LANES), F32),
        ],
        compiler_params=pltpu.CompilerParams(
            dimension_semantics=("arbitrary", "arbitrary"), vmem_limit_bytes=VMEM_LIMIT),
        name="dilated_attn",
    )(rel_bias, q, k, v)


def _ffn_kernel(x_ref, a_ref, b_ref, wo_ref, gpost_ref, gpre2_ref, wg_ref, wu_ref, cw_ref, cb_ref,
                wd_ref, gpost2_ref, o_ref, gbuf, carry, act_buf, *, tiles_per_seq):
    halo = carry.shape[1]
    n_chunks = wd_ref.shape[0] // FF_CHUNK
    n_pass = x_ref.shape[0] // ROWS_PER_PASS

    @pl.when(pl.program_id(0) % tiles_per_seq == 0)
    def _():
        carry[...] = jnp.zeros(carry.shape, F32)

    def pass_rows(p):
        return pl.ds(p * ROWS_PER_PASS, ROWS_PER_PASS)

    def mix_in(p):
        rows = pass_rows(p)
        y1 = (_dot(a_ref[rows, :], wo_ref[0:A_WIDTH, :])
              + _dot(b_ref[rows, :].astype(BF16), wo_ref[A_WIDTH:, :]))
        x1 = x_ref[rows, :] + _rms(y1, gpost_ref[...])
        return x1, _rms(x1, gpre2_ref[...]).astype(BF16)

    def gated(p, h2):
        def gate_up(c):
            cols = slice(c * FF_CHUNK, (c + 1) * FF_CHUNK)
            return _dot(h2, wg_ref[:, cols]), _dot(h2, wu_ref[:, cols])

        gu = gate_up(0)
        for c in range(n_chunks):
            cols = slice(c * FF_CHUNK, (c + 1) * FF_CHUNK)
            gu_next = gate_up(c + 1) if c + 1 < n_chunks else None
            g, up = gu
            gbuf[p, 0:halo, :] = carry[c]
            gbuf[p, halo:halo + ROWS_PER_PASS, :] = g
            carry[c] = g[ROWS_PER_PASS - halo:, :]
            cw = cw_ref[:, cols]
            cv = (cb_ref[:, cols]
                  + gbuf[p, halo - 2:halo - 2 + ROWS_PER_PASS, :] * cw[0:1]
                  + gbuf[p, halo - 1:halo - 1 + ROWS_PER_PASS, :] * cw[1:2]
                  + g * cw[2:3])
            act_buf[pass_rows(p), cols] = (jax.nn.gelu(cv) * up).astype(BF16)
            gu = gu_next

    x1, h2 = mix_in(0)
    for p in range(n_pass):
        gated(p, h2)
        nxt = mix_in(p + 1) if p + 1 < n_pass else None
        rows = pass_rows(p)
        o_ref[rows, :] = x1 + _rms(_dot(act_buf[rows, :], wd_ref[...]), gpost2_ref[...])
        if nxt is not None:
            x1, h2 = nxt


def _ffn_call(x2, a, b, wo, gpost, gpre2, wg, wu, cw, cb, wd, gpost2, seq):
    n, d = x2.shape
    tm = TOKENS_PER_STEP
    ff = wd.shape[0]
    halo = 8
    const = lambda w0, w1: pl.BlockSpec((w0, w1), lambda i: (0, 0), pipeline_mode=pl.Buffered(1))
    tile = lambda w: pl.BlockSpec((tm, w), lambda i: (i, 0))
    return pl.pallas_call(
        functools.partial(_ffn_kernel, tiles_per_seq=seq // tm),
        grid=(n // tm,),
        in_specs=[
            tile(d), tile(A_WIDTH), tile(B_WIDTH),
            const(d, d), const(1, d), const(1, d), const(d, ff), const(d, ff),
            const(CONV_WIDTH, ff), const(1, ff), const(ff, d), const(1, d),
        ],
        out_specs=tile(d),
        out_shape=jax.ShapeDtypeStruct((n, d), F32),
        scratch_shapes=[
            pltpu.VMEM((tm // ROWS_PER_PASS, halo + ROWS_PER_PASS, FF_CHUNK), F32),
            pltpu.VMEM((ff // FF_CHUNK, halo, FF_CHUNK), F32),
            pltpu.VMEM((tm, ff), BF16),
        ],
        compiler_params=pltpu.CompilerParams(
            dimension_semantics=("arbitrary",), vmem_limit_bytes=VMEM_LIMIT),
        name="outproj_ffn",
    )(x2, a, b, wo, gpost, gpre2, wg, wu, cw, cb, wd, gpost2)


def kernel(x, norm_mix_pre, norm_mix_post, norm_ffn_pre, norm_ffn_post, w_in, ln_v_gain, ln_v_bias,
           spatial_w, spatial_b, rel_bias, w_out, w_gate, w_up, conv_w, conv_b, w_down):
    batch, seq, d = x.shape
    depth = w_in.shape[0]
    ff = w_gate.shape[-1]
    assert seq % TOKENS_PER_STEP == 0 and ff % FF_CHUNK == 0 and d == A_WIDTH + B_WIDTH
    assert seq // KEYS_BACK == max(DILATIONS)

    gmat = jnp.asarray(np.kron(np.eye(A_GROUPS), np.ones((HEAD_DIM, HEAD_DIM))), BF16)
    x2 = x.reshape(batch * seq, d)
    for l in range(depth):
        row = lambda p: p[l].reshape(1, -1).astype(F32)
        sb = jnp.repeat(spatial_b[l].astype(F32).T, HEAD_DIM, axis=1)
        a_out, q, k, v = _proj_call(
            x2, row(norm_mix_pre), w_in[l].astype(BF16), gmat,
            row(ln_v_gain), row(ln_v_bias), spatial_w[l].astype(F32), sb)
        b_out = _attn_call(rel_bias.astype(F32), q, k, v, batch, seq)
        x2 = _ffn_call(
            x2, a_out, b_out, w_out[l].astype(BF16), row(norm_mix_post), row(norm_ffn_pre),
            w_gate[l].astype(BF16), w_up[l].astype(BF16), conv_w[l].astype(F32), row(conv_b),
            w_down[l].astype(BF16),
            row(norm_ffn_post), seq)
    return x2.reshape(batch, seq, d)
```

```python
import functools
import math

import numpy as np
import jax
import jax.numpy as jnp
from jax import lax
from jax.experimental import pallas as pl
from jax.experimental.pallas import tpu as pltpu

F32 = jnp.float32
BF16 = jnp.bfloat16

HEAD_DIM = 64
A_GROUPS = 4
A_WIDTH = A_GROUPS * HEAD_DIM
B_HEADS = 12
B_WIDTH = B_HEADS * HEAD_DIM
CHUNK = 128
DILATIONS = (1, 4, 16)
KEYS_BACK = 128
NUM_BUCKETS = 32
MAX_DISTANCE = 2048
CONV_WIDTH = 3
NORM_EPS = 1e-6
NEG_INF = -1e30
LOG2E = 1.4426950408889634

LANES = 128
HEADS_PER_STEP = LANES // HEAD_DIM
FF_CHUNK = 256
PROJ_TOKENS_PER_STEP = 1024
FFN_TOKENS_PER_STEP = 512
ROWS_PER_PASS = 512
VMEM_LIMIT = 56 * 1024 * 1024


def _rms(x, g):
    ms = jnp.mean(x * x, axis=-1, keepdims=True)
    return x * lax.rsqrt(ms + NORM_EPS) * g


def _dot(a, b):
    return jnp.dot(a, b, preferred_element_type=F32)


def _dot_nt(a, b):
    return lax.dot_general(a, b, (((1,), (1,)), ((), ())), preferred_element_type=F32)


def _proj_kernel(x_ref, g_ref, w_ref, gmat_ref, lng_ref, lnb_ref, sw_ref, sb_ref,
                 a_ref, q_ref, k_ref, v_ref):
    gmat = gmat_ref[...]
    row = lax.broadcasted_iota(jnp.int32, (CHUNK, CHUNK), 0)
    col = lax.broadcasted_iota(jnp.int32, (CHUNK, CHUNK), 1)
    tril = row >= col
    ws = [jnp.where(tril, sw_ref[g], 0.0).astype(BF16) for g in range(A_GROUPS)]
    grp = lax.broadcasted_iota(jnp.int32, (CHUNK, A_WIDTH), 1) // HEAD_DIM
    sb = sb_ref[...]
    o = 2 * A_WIDTH

    def group_sum(t):
        hi = t.astype(BF16)
        lo = (t - hi.astype(F32)).astype(BF16)
        return _dot(hi, gmat) + _dot(lo, gmat)

    for p in range(x_ref.shape[0] // ROWS_PER_PASS):
        rows = pl.ds(p * ROWS_PER_PASS, ROWS_PER_PASS)
        h = _rms(x_ref[rows, :], g_ref[...]).astype(BF16)

        q_ref[rows, :] = _dot(h, w_ref[:, o:o + B_WIDTH]) * (LOG2E / math.sqrt(HEAD_DIM))
        k_ref[rows, :] = _dot(h, w_ref[:, o + B_WIDTH:o + 2 * B_WIDTH])
        v_ref[rows, :] = _dot(h, w_ref[:, o + 2 * B_WIDTH:o + 3 * B_WIDTH])

        uv = _dot(h, w_ref[:, 0:o])
        u = jax.nn.gelu(uv[:, :A_WIDTH])
        vf = jax.nn.gelu(uv[:, A_WIDTH:])
        mu = group_sum(vf) * (1.0 / HEAD_DIM)
        dl = vf - mu
        var = group_sum(dl * dl) * (1.0 / HEAD_DIM)
        vn = (dl * lax.rsqrt(var + NORM_EPS) * lng_ref[...] + lnb_ref[...]).astype(BF16)

        for c in range(ROWS_PER_PASS // CHUNK):
            chunk = slice(c * CHUNK, (c + 1) * CHUNK)
            vc = vn[chunk]
            z = _dot(ws[0], vc)
            for g in range(1, A_GROUPS):
                z = jnp.where(grp == g, _dot(ws[g], vc), z)
            a_ref[pl.ds(p * ROWS_PER_PASS + c * CHUNK, CHUNK), :] = (u[chunk] * (z + sb)).astype(BF16)


def _proj_call(x2, g_pre, w_in, gmat, lng, lnb, sw, sb):
    n, d = x2.shape
    tm = PROJ_TOKENS_PER_STEP
    cols = w_in.shape[1]
    const = lambda i: (0, 0)
    once = pl.Buffered(1)
    return pl.pallas_call(
        _proj_kernel,
        grid=(n // tm,),
        in_specs=[
            pl.BlockSpec((tm, d), lambda i: (i, 0)),
            pl.BlockSpec((1, d), const, pipeline_mode=once),
            pl.BlockSpec((d, cols), const, pipeline_mode=once),
            pl.BlockSpec((A_WIDTH, A_WIDTH), const, pipeline_mode=once),
            pl.BlockSpec((1, A_WIDTH), const, pipeline_mode=once),
            pl.BlockSpec((1, A_WIDTH), const, pipeline_mode=once),
            pl.BlockSpec((A_GROUPS, CHUNK, CHUNK), lambda i: (0, 0, 0), pipeline_mode=once),
            pl.BlockSpec((CHUNK, A_WIDTH), const, pipeline_mode=once),
        ],
        out_specs=[
            pl.BlockSpec((tm, A_WIDTH), lambda i: (i, 0)),
            pl.BlockSpec((tm, B_WIDTH), lambda i: (i, 0)),
            pl.BlockSpec((tm, B_WIDTH), lambda i: (i, 0)),
            pl.BlockSpec((tm, B_WIDTH), lambda i: (i, 0)),
        ],
        out_shape=[
            jax.ShapeDtypeStruct((n, A_WIDTH), BF16),
            jax.ShapeDtypeStruct((n, B_WIDTH), F32),
            jax.ShapeDtypeStruct((n, B_WIDTH), F32),
            jax.ShapeDtypeStruct((n, B_WIDTH), F32),
        ],
        compiler_params=pltpu.CompilerParams(
            dimension_semantics=("arbitrary",), vmem_limit_bytes=VMEM_LIMIT),
        name="proj_gmlp",
    )(x2, g_pre, w_in, gmat, lng, lnb, sw, sb)


def _t5_bucket_np(dist):
    max_exact = NUM_BUCKETS // 2
    d = np.maximum(dist, 1).astype(np.float64)
    large = max_exact + (np.log(d / max_exact) / math.log(MAX_DISTANCE / max_exact)
                         * (NUM_BUCKETS - max_exact))
    large = np.minimum(large.astype(np.int32), NUM_BUCKETS - 1)
    return np.where(dist < max_exact, dist, large)


def _bucket_starts(d):
    b = _t5_bucket_np(np.arange(KEYS_BACK + 1) * d)
    return [(r, int(b[r])) for r in range(KEYS_BACK + 1) if r == 0 or b[r] != b[r - 1]]


def _attn_kernel(rb_ref, q_ref, k_ref, v_ref, *rest):
    n_res = DILATIONS[1]
    res_refs = [rest[3 * r:3 * r + 3] for r in range(n_res)]
    o_ref, qs, kp, va, bt, mb, lb, ab = rest[3 * n_res:]
    seq = q_ref.shape[0]
    nblk = seq // KEYS_BACK
    n_blocks = len(DILATIONS) * nblk
    quarter = seq // n_res
    assert DILATIONS == (1, 4, 16) and nblk == DILATIONS[-1]
    lane = lax.broadcasted_iota(jnp.int32, (KEYS_BACK, LANES), 1)
    first_head = lane < HEAD_DIM

    @pl.when(pl.program_id(1) == 0)
    def _():
        i = lax.broadcasted_iota(jnp.int32, (KEYS_BACK, 2 * KEYS_BACK), 0)
        j = lax.broadcasted_iota(jnp.int32, (KEYS_BACK, 2 * KEYS_BACK), 1)
        for di, d in enumerate(DILATIONS):
            for hh in range(HEADS_PER_STEP):
                head = pl.program_id(0) * HEADS_PER_STEP + hh
                rows = pl.ds(hh * KEYS_BACK, KEYS_BACK)
                for variant, rel in enumerate((KEYS_BACK + i - j, i - j)):
                    val = jnp.zeros(rel.shape, F32)
                    for start, bucket in _bucket_starts(d):
                        val = jnp.where(rel >= start, rb_ref[bucket, head], val)
                    band = (rel >= 0) & (rel <= KEYS_BACK)
                    bt[2 * di + variant, rows, :] = jnp.where(band, val * LOG2E, NEG_INF)

    def block_rows(j):
        return pl.ds(j * KEYS_BACK, KEYS_BACK)

    @pl.when((pl.program_id(0) == 0) & (pl.program_id(1) == 0))
    def _():
        kp[...] = jnp.zeros(kp.shape, BF16)
        rows_total = va.shape[1]
        lane_all = lax.broadcasted_iota(jnp.int32, (rows_total, LANES), 1)
        for hh in range(HEADS_PER_STEP):
            own = (lane_all // HEAD_DIM) == hh
            va[hh, :, 0:LANES] = jnp.zeros((rows_total, LANES), BF16)
            va[hh, :, LANES:] = jnp.where(own, 1.0, 0.0).astype(BF16)

    def regroup(j):
        di, g = j // nblk, j % nblk
        if di == 0:
            srcs, rows = (q_ref, k_ref, v_ref), block_rows(g)
        elif di == 1:
            srcs, rows = res_refs[g // n_res], block_rows(g % n_res)
        else:
            srcs, rows = res_refs[g % n_res], pl.ds(g // n_res, KEYS_BACK, stride=n_res)
        qv, kv, vv = (ref[rows, :] for ref in srcs)
        qs[j, 0:KEYS_BACK, :] = jnp.where(first_head, qv, 0.0).astype(BF16)
        qs[j, KEYS_BACK:, :] = jnp.where(first_head, 0.0, qv).astype(BF16)
        kp[block_rows(j), :] = kv.astype(BF16)
        va[0, block_rows(j), 0:LANES] = jnp.where(first_head, vv, 0.0).astype(BF16)
        va[1, block_rows(j), 0:LANES] = jnp.where(first_head, 0.0, vv).astype(BF16)

    def out_rows(j):
        if j // nblk == 2:
            r = j % nblk
            return pl.ds(2 * seq + (r % n_res) * quarter + r // n_res, KEYS_BACK, stride=n_res)
        return block_rows(j)

    def attend(j):
        di, g = j // nblk, j % nblk
        no_prev = g % (nblk // DILATIONS[di]) == 0
        keys = pl.ds((j if no_prev else j - 1) * KEYS_BACK, 2 * KEYS_BACK)
        s = _dot_nt(qs[j], kp[keys, :]) + bt[2 * di + int(no_prev)]
        m = jnp.max(s, axis=-1, keepdims=True)
        mb[out_rows(j), :] = jnp.where(first_head, m[:KEYS_BACK], m[KEYS_BACK:])
        p = jnp.exp2(s - m).astype(BF16)
        p = jnp.concatenate([p[:KEYS_BACK], p[KEYS_BACK:]], axis=1)
        vals = jnp.concatenate([va[0, keys, :], va[1, keys, :]], axis=0)
        r = _dot(p, vals)
        ab[out_rows(j), :] = r[:, :LANES]
        lb[out_rows(j), :] = r[:, LANES:]

    def combine(c):
        tok = pl.ds(c // n_res + (c % n_res) * KEYS_BACK * n_res, KEYS_BACK, stride=n_res)
        v0, v1, v2 = tok, block_rows(nblk + c), block_rows(2 * nblk + c)
        ms = [mb[v, :] for v in (v0, v1, v2)]
        mx = jnp.maximum(jnp.maximum(ms[0], ms[1]), ms[2])
        e0, e1, e2 = (jnp.exp2(m - mx) for m in ms)
        num = e0 * ab[v0, :] + e1 * ab[v1, :] + e2 * ab[v2, :]
        den = e0 * lb[v0, :] + e1 * lb[v1, :] + e2 * lb[v2, :]
        o_ref[tok, :] = num / den

    d16 = [2 * nblk + r for k in range(n_res) for r in range(k, nblk, n_res)]
    order = d16 + list(range(2 * nblk))
    ahead = 3
    for j in order[:ahead]:
        regroup(j)
    for i, j in enumerate(order):
        if i + ahead < n_blocks:
            regroup(order[i + ahead])
        attend(j)
        if j // nblk == 1:
            combine(j % nblk)


def _attn_call(rel_bias, q, k, v, batch, seq):
    nd = len(DILATIONS)
    nblk = seq // KEYS_BACK
    n_res = DILATIONS[1]
    n_pairs = B_HEADS // HEADS_PER_STEP
    io_spec = pl.BlockSpec((seq, LANES), lambda hp, b: (b, hp))
    quarter_view = lambda t: t.reshape(batch, seq // n_res, n_res * B_WIDTH)
    res_spec = lambda r: pl.BlockSpec((None, seq // n_res, LANES),
                                      lambda hp, b: (b, 0, r * n_pairs + hp))
    res_inputs, res_specs = [], []
    for r in range(n_res):
        res_inputs += [quarter_view(q), quarter_view(k), quarter_view(v)]
        res_specs += [res_spec(r)] * 3
    return pl.pallas_call(
        _attn_kernel,
        grid=(n_pairs, batch),
        in_specs=[pl.BlockSpec(memory_space=pltpu.SMEM), io_spec, io_spec, io_spec] + res_specs,
        out_specs=io_spec,
        out_shape=jax.ShapeDtypeStruct((batch * seq, B_WIDTH), F32),
        scratch_shapes=[
            pltpu.VMEM((nd * nblk, 2 * KEYS_BACK, LANES), BF16),
            pltpu.VMEM((nd * seq + KEYS_BACK, LANES), BF16),
            pltpu.VMEM((HEADS_PER_STEP, nd * seq + KEYS_BACK, 2 * LANES), BF16),
            pltpu.VMEM((2 * nd, 2 * KEYS_BACK, 2 * KEYS_BACK), F32),
            pltpu.VMEM((nd * seq, LANES), F32),
            pltpu.VMEM((nd * seq, LANES), F32),
            pltpu.VMEM((nd * seq, LANES), F32),
        ],
        compiler_params=pltpu.CompilerParams(
            dimension_semantics=("arbitrary", "arbitrary"), vmem_limit_bytes=VMEM_LIMIT),
        name="dilated_attn",
    )(rel_bias, q, k, v, *res_inputs)


def _ffn_kernel(x_ref, a_ref, b_ref, wo_ref, gpost_ref, gpre2_ref, wg_ref, wu_ref, cw_ref, cb_ref,
                wd_ref, gpost2_ref, o_ref, gbuf, carry, act_buf, *, tiles_per_seq):
    halo = carry.shape[1]
    n_chunks = wd_ref.shape[0] // FF_CHUNK
    n_pass = x_ref.shape[0] // ROWS_PER_PASS

    @pl.when(pl.program_id(0) % tiles_per_seq == 0)
    def _():
        carry[...] = jnp.zeros(carry.shape, F32)

    def pass_rows(p):
        return pl.ds(p * ROWS_PER_PASS, ROWS_PER_PASS)

    def mix_in(p):
        rows = pass_rows(p)
        y1 = (_dot(a_ref[rows, :], wo_ref[0:A_WIDTH, :])
              + _dot(b_ref[rows, :].astype(BF16), wo_ref[A_WIDTH:, :]))
        x1 = x_ref[rows, :] + _rms(y1, gpost_ref[...])
        return x1, _rms(x1, gpre2_ref[...]).astype(BF16)

    def gated(p, h2):
        def gate_up(c):
            cols = slice(c * FF_CHUNK, (c + 1) * FF_CHUNK)
            return _dot(h2, wg_ref[:, cols]), _dot(h2, wu_ref[:, cols])

        gu = gate_up(0)
        for c in range(n_chunks):
            cols = slice(c * FF_CHUNK, (c + 1) * FF_CHUNK)
            gu_next = gate_up(c + 1) if c + 1 < n_chunks else None
            g, up = gu
            gbuf[p, 0:halo, :] = carry[c]
            gbuf[p, halo:halo + ROWS_PER_PASS, :] = g
            carry[c] = g[ROWS_PER_PASS - halo:, :]
            cw = cw_ref[:, cols]
            cv = (cb_ref[:, cols]
                  + gbuf[p, halo - 2:halo - 2 + ROWS_PER_PASS, :] * cw[0:1]
                  + gbuf[p, halo - 1:halo - 1 + ROWS_PER_PASS, :] * cw[1:2]
                  + g * cw[2:3])
            act_buf[pass_rows(p), cols] = (jax.nn.gelu(cv) * up).astype(BF16)
            gu = gu_next

    x1, h2 = mix_in(0)
    for p in range(n_pass):
        gated(p, h2)
        nxt = mix_in(p + 1) if p + 1 < n_pass else None
        rows = pass_rows(p)
        o_ref[rows, :] = x1 + _rms(_dot(act_buf[rows, :], wd_ref[...]), gpost2_ref[...])
        if nxt is not None:
            x1, h2 = nxt


def _ffn_call(x2, a, b, wo, gpost, gpre2, wg, wu, cw, cb, wd, gpost2, seq):
    n, d = x2.shape
    tm = FFN_TOKENS_PER_STEP
    ff = wd.shape[0]
    halo = 8
    const = lambda w0, w1: pl.BlockSpec((w0, w1), lambda i: (0, 0), pipeline_mode=pl.Buffered(1))
    tile = lambda w: pl.BlockSpec((tm, w), lambda i: (i, 0))
    return pl.pallas_call(
        functools.partial(_ffn_kernel, tiles_per_seq=seq // tm),
        grid=(n // tm,),
        in_specs=[
            tile(d), tile(A_WIDTH), tile(B_WIDTH),
            const(d, d), const(1, d), const(1, d), const(d, ff), const(d, ff),
            const(CONV_WIDTH, ff), const(1, ff), const(ff, d), const(1, d),
        ],
        out_specs=tile(d),
        out_shape=jax.ShapeDtypeStruct((n, d), F32),
        scratch_shapes=[
            pltpu.VMEM((tm // ROWS_PER_PASS, halo + ROWS_PER_PASS, FF_CHUNK), F32),
            pltpu.VMEM((ff // FF_CHUNK, halo, FF_CHUNK), F32),
            pltpu.VMEM((tm, ff), BF16),
        ],
        compiler_params=pltpu.CompilerParams(
            dimension_semantics=("arbitrary",), vmem_limit_bytes=VMEM_LIMIT),
        name="outproj_ffn",
    )(x2, a, b, wo, gpost, gpre2, wg, wu, cw, cb, wd, gpost2)


def kernel(x, norm_mix_pre, norm_mix_post, norm_ffn_pre, norm_ffn_post, w_in, ln_v_gain, ln_v_bias,
           spatial_w, spatial_b, rel_bias, w_out, w_gate, w_up, conv_w, conv_b, w_down):
    batch, seq, d = x.shape
    depth = w_in.shape[0]
    ff = w_gate.shape[-1]
    assert seq % PROJ_TOKENS_PER_STEP == 0 and seq % FFN_TOKENS_PER_STEP == 0
    assert ff % FF_CHUNK == 0 and d == A_WIDTH + B_WIDTH
    assert seq // KEYS_BACK == max(DILATIONS)

    gmat = jnp.asarray(np.kron(np.eye(A_GROUPS), np.ones((HEAD_DIM, HEAD_DIM))), BF16)
    x2 = x.reshape(batch * seq, d)
    for l in range(depth):
        row = lambda p: p[l].reshape(1, -1).astype(F32)
        sb = jnp.repeat(spatial_b[l].astype(F32).T, HEAD_DIM, axis=1)
        a_out, q, k, v = _proj_call(
            x2, row(norm_mix_pre), w_in[l].astype(BF16), gmat,
            row(ln_v_gain), row(ln_v_bias), spatial_w[l].astype(F32), sb)
        b_out = _attn_call(rel_bias.astype(F32), q, k, v, batch, seq)
        x2 = _ffn_call(
            x2, a_out, b_out, w_out[l].astype(BF16), row(norm_mix_post), row(norm_ffn_pre),
            w_gate[l].astype(BF16), w_up[l].astype(BF16), conv_w[l].astype(F32), row(conv_b),
            w_down[l].astype(BF16), row(norm_ffn_post), seq)
    return x2.reshape(batch, seq, d)
```

```python
import functools
import math

import numpy as np
import jax
import jax.numpy as jnp
from jax import lax
from jax.experimental import pallas as pl
from jax.experimental.pallas import tpu as pltpu

F32 = jnp.float32
BF16 = jnp.bfloat16

HEAD_DIM = 64
A_GROUPS = 4
A_WIDTH = A_GROUPS * HEAD_DIM
B_HEADS = 12
B_WIDTH = B_HEADS * HEAD_DIM
CHUNK = 128
DILATIONS = (1, 4, 16)
KEYS_BACK = 128
NUM_BUCKETS = 32
MAX_DISTANCE = 2048
CONV_WIDTH = 3
NORM_EPS = 1e-6
NEG_INF = -1e30
LOG2E = 1.4426950408889634

LANES = 128
HEADS_PER_STEP = LANES // HEAD_DIM
HEAD_PAIRS = B_HEADS // HEADS_PER_STEP
FF_CHUNK = 256
PROJ_TOKENS_PER_STEP = 1024
FFN_TOKENS_PER_STEP = 512
ROWS_PER_PASS = 512
VMEM_LIMIT = 56 * 1024 * 1024


def _rms(x, g):
    ms = jnp.mean(x * x, axis=-1, keepdims=True)
    return x * lax.rsqrt(ms + NORM_EPS) * g


def _dot(a, b):
    return jnp.dot(a, b, preferred_element_type=F32)


def _dot_nt(a, b):
    return lax.dot_general(a, b, (((1,), (1,)), ((), ())), preferred_element_type=F32)


def _proj_kernel(x_ref, g_ref, w_ref, gmat_ref, lng_ref, lnb_ref, sw_ref, sb_ref,
                 a_ref, q_ref, k_ref, v_ref):
    gmat = gmat_ref[...]
    row = lax.broadcasted_iota(jnp.int32, (CHUNK, CHUNK), 0)
    col = lax.broadcasted_iota(jnp.int32, (CHUNK, CHUNK), 1)
    tril = row >= col
    ws = [jnp.where(tril, sw_ref[g], 0.0).astype(BF16) for g in range(A_GROUPS)]
    grp = lax.broadcasted_iota(jnp.int32, (CHUNK, A_WIDTH), 1) // HEAD_DIM
    sb = sb_ref[...]
    o = 2 * A_WIDTH

    def group_sum(t):
        hi = t.astype(BF16)
        lo = (t - hi.astype(F32)).astype(BF16)
        return _dot(hi, gmat) + _dot(lo, gmat)

    for p in range(x_ref.shape[0] // ROWS_PER_PASS):
        rows = pl.ds(p * ROWS_PER_PASS, ROWS_PER_PASS)
        h = _rms(x_ref[rows, :], g_ref[...]).astype(BF16)

        for n, ref in enumerate((q_ref, k_ref, v_ref)):
            val = _dot(h, w_ref[:, o + n * B_WIDTH:o + (n + 1) * B_WIDTH])
            if n == 0:
                val = val * (LOG2E / math.sqrt(HEAD_DIM))
            for hp in range(HEAD_PAIRS):
                ref[hp, rows, :] = val[:, hp * LANES:(hp + 1) * LANES]

        uv = _dot(h, w_ref[:, 0:o])
        u = jax.nn.gelu(uv[:, :A_WIDTH])
        vf = jax.nn.gelu(uv[:, A_WIDTH:])
        mu = group_sum(vf) * (1.0 / HEAD_DIM)
        dl = vf - mu
        var = group_sum(dl * dl) * (1.0 / HEAD_DIM)
        vn = (dl * lax.rsqrt(var + NORM_EPS) * lng_ref[...] + lnb_ref[...]).astype(BF16)

        for c in range(ROWS_PER_PASS // CHUNK):
            chunk = slice(c * CHUNK, (c + 1) * CHUNK)
            vc = vn[chunk]
            z = _dot(ws[0], vc)
            for g in range(1, A_GROUPS):
                z = jnp.where(grp == g, _dot(ws[g], vc), z)
            a_ref[pl.ds(p * ROWS_PER_PASS + c * CHUNK, CHUNK), :] = (u[chunk] * (z + sb)).astype(BF16)


def _proj_call(x2, g_pre, w_in, gmat, lng, lnb, sw, sb):
    n, d = x2.shape
    tm = PROJ_TOKENS_PER_STEP
    cols = w_in.shape[1]
    const = lambda i: (0, 0)
    once = pl.Buffered(1)
    return pl.pallas_call(
        _proj_kernel,
        grid=(n // tm,),
        in_specs=[
            pl.BlockSpec((tm, d), lambda i: (i, 0)),
            pl.BlockSpec((1, d), const, pipeline_mode=once),
            pl.BlockSpec((d, cols), const, pipeline_mode=once),
            pl.BlockSpec((A_WIDTH, A_WIDTH), const, pipeline_mode=once),
            pl.BlockSpec((1, A_WIDTH), const, pipeline_mode=once),
            pl.BlockSpec((1, A_WIDTH), const, pipeline_mode=once),
            pl.BlockSpec((A_GROUPS, CHUNK, CHUNK), lambda i: (0, 0, 0), pipeline_mode=once),
            pl.BlockSpec((CHUNK, A_WIDTH), const, pipeline_mode=once),
        ],
        out_specs=[
            pl.BlockSpec((tm, A_WIDTH), lambda i: (i, 0)),
            pl.BlockSpec((HEAD_PAIRS, tm, LANES), lambda i: (0, i, 0)),
            pl.BlockSpec((HEAD_PAIRS, tm, LANES), lambda i: (0, i, 0)),
            pl.BlockSpec((HEAD_PAIRS, tm, LANES), lambda i: (0, i, 0)),
        ],
        out_shape=[
            jax.ShapeDtypeStruct((n, A_WIDTH), BF16),
            jax.ShapeDtypeStruct((HEAD_PAIRS, n, LANES), F32),
            jax.ShapeDtypeStruct((HEAD_PAIRS, n, LANES), F32),
            jax.ShapeDtypeStruct((HEAD_PAIRS, n, LANES), F32),
        ],
        compiler_params=pltpu.CompilerParams(
            dimension_semantics=("arbitrary",), vmem_limit_bytes=VMEM_LIMIT),
        name="proj_gmlp",
    )(x2, g_pre, w_in, gmat, lng, lnb, sw, sb)


def _t5_bucket_np(dist):
    max_exact = NUM_BUCKETS // 2
    d = np.maximum(dist, 1).astype(np.float64)
    large = max_exact + (np.log(d / max_exact) / math.log(MAX_DISTANCE / max_exact)
                         * (NUM_BUCKETS - max_exact))
    large = np.minimum(large.astype(np.int32), NUM_BUCKETS - 1)
    return np.where(dist < max_exact, dist, large)


def _bucket_starts(d):
    b = _t5_bucket_np(np.arange(KEYS_BACK + 1) * d)
    return [(r, int(b[r])) for r in range(KEYS_BACK + 1) if r == 0 or b[r] != b[r - 1]]


def _attn_kernel(rb_ref, q_ref, k_ref, v_ref, o_ref, qs, kp, va, bt, stg, mb, lb, ab):
    n_res = DILATIONS[1]
    seq = q_ref.shape[0]
    nblk = seq // KEYS_BACK
    n_blocks = len(DILATIONS) * nblk
    quarter = seq // n_res
    assert DILATIONS == (1, 4, 16) and nblk == DILATIONS[-1]
    lane = lax.broadcasted_iota(jnp.int32, (KEYS_BACK, LANES), 1)
    first_head = lane < HEAD_DIM

    @pl.when(pl.program_id(1) == 0)
    def _():
        i = lax.broadcasted_iota(jnp.int32, (KEYS_BACK, 2 * KEYS_BACK), 0)
        j = lax.broadcasted_iota(jnp.int32, (KEYS_BACK, 2 * KEYS_BACK), 1)
        for di, d in enumerate(DILATIONS):
            for hh in range(HEADS_PER_STEP):
                head = pl.program_id(0) * HEADS_PER_STEP + hh
                rows = pl.ds(hh * KEYS_BACK, KEYS_BACK)
                for variant, rel in enumerate((KEYS_BACK + i - j, i - j)):
                    val = jnp.zeros(rel.shape, F32)
                    for start, bucket in _bucket_starts(d):
                        val = jnp.where(rel >= start, rb_ref[bucket, head], val)
                    band = (rel >= 0) & (rel <= KEYS_BACK)
                    bt[2 * di + variant, rows, :] = jnp.where(band, val * LOG2E, NEG_INF)

    def block_rows(j):
        return pl.ds(j * KEYS_BACK, KEYS_BACK)

    @pl.when((pl.program_id(0) == 0) & (pl.program_id(1) == 0))
    def _():
        kp[...] = jnp.zeros(kp.shape, BF16)
        rows_total = va.shape[1]
        lane_all = lax.broadcasted_iota(jnp.int32, (rows_total, LANES), 1)
        for hh in range(HEADS_PER_STEP):
            own = (lane_all // HEAD_DIM) == hh
            va[hh, :, 0:LANES] = jnp.zeros((rows_total, LANES), BF16)
            va[hh, :, LANES:] = jnp.where(own, 1.0, 0.0).astype(BF16)

    def regroup(j):
        di, g = j // nblk, j % nblk
        if di == 0:
            qv, kv, vv = (ref[block_rows(g), :] for ref in (q_ref, k_ref, v_ref))
        elif di == 1:
            tok = pl.ds(g // n_res + (g % n_res) * KEYS_BACK * n_res, KEYS_BACK, stride=n_res)
            qv, kv, vv = (ref[tok, :] for ref in (q_ref, k_ref, v_ref))
            for n, val in enumerate((qv, kv, vv)):
                stg[n, block_rows(g), :] = val
        else:
            rows = pl.ds((g % n_res) * quarter + g // n_res, KEYS_BACK, stride=n_res)
            qv, kv, vv = (stg[n, rows, :] for n in range(3))
        qs[j, 0:KEYS_BACK, :] = jnp.where(first_head, qv, 0.0).astype(BF16)
        qs[j, KEYS_BACK:, :] = jnp.where(first_head, 0.0, qv).astype(BF16)
        kp[block_rows(j), :] = kv.astype(BF16)
        va[0, block_rows(j), 0:LANES] = jnp.where(first_head, vv, 0.0).astype(BF16)
        va[1, block_rows(j), 0:LANES] = jnp.where(first_head, 0.0, vv).astype(BF16)

    def out_rows(j):
        if j // nblk == 2:
            r = j % nblk
            return pl.ds(2 * seq + (r % n_res) * quarter + r // n_res, KEYS_BACK, stride=n_res)
        return block_rows(j)

    def attend(j):
        di, g = j // nblk, j % nblk
        no_prev = g % (nblk // DILATIONS[di]) == 0
        keys = pl.ds((j if no_prev else j - 1) * KEYS_BACK, 2 * KEYS_BACK)
        s = _dot_nt(qs[j], kp[keys, :]) + bt[2 * di + int(no_prev)]
        m = jnp.max(s, axis=-1, keepdims=True)
        mb[out_rows(j), :] = jnp.where(first_head, m[:KEYS_BACK], m[KEYS_BACK:])
        p = jnp.exp2(s - m).astype(BF16)
        p = jnp.concatenate([p[:KEYS_BACK], p[KEYS_BACK:]], axis=1)
        vals = jnp.concatenate([va[0, keys, :], va[1, keys, :]], axis=0)
        r = _dot(p, vals)
        ab[out_rows(j), :] = r[:, :LANES]
        lb[out_rows(j), :] = r[:, LANES:]

    def combine(c):
        tok = pl.ds(c // n_res + (c % n_res) * KEYS_BACK * n_res, KEYS_BACK, stride=n_res)
        v0, v1, v2 = tok, block_rows(nblk + c), block_rows(2 * nblk + c)
        ms = [mb[v, :] for v in (v0, v1, v2)]
        mx = jnp.maximum(jnp.maximum(ms[0], ms[1]), ms[2])
        e0, e1, e2 = (jnp.exp2(m - mx) for m in ms)
        num = e0 * ab[v0, :] + e1 * ab[v1, :] + e2 * ab[v2, :]
        den = e0 * lb[v0, :] + e1 * lb[v1, :] + e2 * lb[v2, :]
        o_ref[tok, :] = num / den

    d4 = [nblk + g for g in range(nblk)]
    d16 = [[2 * nblk + r for r in range(k, nblk, n_res)] for k in range(n_res)]
    regroup_order = ([j for k in range(n_res) for j in d4[n_res * k:n_res * (k + 1)] + d16[k]]
                     + list(range(nblk)))
    attend_order = [j for grp in d16 for j in grp] + list(range(nblk)) + d4
    ahead = 3
    done = 0
    for j in attend_order:
        want = min(regroup_order.index(j) + 1 + ahead, n_blocks)
        for jj in regroup_order[done:want]:
            regroup(jj)
        done = max(done, want)
        attend(j)
        if j // nblk == 1:
            combine(j % nblk)


def _attn_call(rel_bias, q, k, v, batch, seq):
    nd = len(DILATIONS)
    nblk = seq // KEYS_BACK
    io_spec = pl.BlockSpec((None, seq, LANES), lambda hp, b: (hp, b, 0))
    return pl.pallas_call(
        _attn_kernel,
        grid=(HEAD_PAIRS, batch),
        in_specs=[pl.BlockSpec(memory_space=pltpu.SMEM), io_spec, io_spec, io_spec],
        out_specs=io_spec,
        out_shape=jax.ShapeDtypeStruct((HEAD_PAIRS, batch * seq, LANES), F32),
        scratch_shapes=[
            pltpu.VMEM((nd * nblk, 2 * KEYS_BACK, LANES), BF16),
            pltpu.VMEM((nd * seq + KEYS_BACK, LANES), BF16),
            pltpu.VMEM((HEADS_PER_STEP, nd * seq + KEYS_BACK, 2 * LANES), BF16),
            pltpu.VMEM((2 * nd, 2 * KEYS_BACK, 2 * KEYS_BACK), F32),
            pltpu.VMEM((3, seq, LANES), F32),
            pltpu.VMEM((nd * seq, LANES), F32),
            pltpu.VMEM((nd * seq, LANES), F32),
            pltpu.VMEM((nd * seq, LANES), F32),
        ],
        compiler_params=pltpu.CompilerParams(
            dimension_semantics=("arbitrary", "arbitrary"), vmem_limit_bytes=VMEM_LIMIT),
        name="dilated_attn",
    )(rel_bias, q, k, v)


def _ffn_kernel(x_ref, a_ref, b_ref, wo_ref, gpost_ref, gpre2_ref, wg_ref, wu_ref, cw_ref, cb_ref,
                wd_ref, gpost2_ref, o_ref, gbuf, carry, act_buf, *, tiles_per_seq):
    halo = carry.shape[1]
    n_chunks = wd_ref.shape[0] // FF_CHUNK
    n_pass = x_ref.shape[0] // ROWS_PER_PASS

    @pl.when(pl.program_id(0) % tiles_per_seq == 0)
    def _():
        carry[...] = jnp.zeros(carry.shape, F32)

    def pass_rows(p):
        return pl.ds(p * ROWS_PER_PASS, ROWS_PER_PASS)

    def mix_in(p):
        rows = pass_rows(p)
        b = jnp.concatenate([b_ref[hp, rows, :] for hp in range(HEAD_PAIRS)], axis=1)
        y1 = _dot(a_ref[rows, :], wo_ref[0:A_WIDTH, :]) + _dot(b.astype(BF16), wo_ref[A_WIDTH:, :])
        x1 = x_ref[rows, :] + _rms(y1, gpost_ref[...])
        return x1, _rms(x1, gpre2_ref[...]).astype(BF16)

    def gated(p, h2):
        def gate_up(c):
            cols = slice(c * FF_CHUNK, (c + 1) * FF_CHUNK)
            return _dot(h2, wg_ref[:, cols]), _dot(h2, wu_ref[:, cols])

        gu = gate_up(0)
        for c in range(n_chunks):
            cols = slice(c * FF_CHUNK, (c + 1) * FF_CHUNK)
            gu_next = gate_up(c + 1) if c + 1 < n_chunks else None
            g, up = gu
            gbuf[p, 0:halo, :] = carry[c]
            gbuf[p, halo:halo + ROWS_PER_PASS, :] = g
            carry[c] = g[ROWS_PER_PASS - halo:, :]
            cw = cw_ref[:, cols]
            cv = (cb_ref[:, cols]
                  + gbuf[p, halo - 2:halo - 2 + ROWS_PER_PASS, :] * cw[0:1]
                  + gbuf[p, halo - 1:halo - 1 + ROWS_PER_PASS, :] * cw[1:2]
                  + g * cw[2:3])
            act_buf[pass_rows(p), cols] = (jax.nn.gelu(cv) * up).astype(BF16)
            gu = gu_next

    x1, h2 = mix_in(0)
    for p in range(n_pass):
        gated(p, h2)
        nxt = mix_in(p + 1) if p + 1 < n_pass else None
        rows = pass_rows(p)
        o_ref[rows, :] = x1 + _rms(_dot(act_buf[rows, :], wd_ref[...]), gpost2_ref[...])
        if nxt is not None:
            x1, h2 = nxt


def _ffn_call(x2, a, b, wo, gpost, gpre2, wg, wu, cw, cb, wd, gpost2, seq):
    n, d = x2.shape
    tm = FFN_TOKENS_PER_STEP
    ff = wd.shape[0]
    halo = 8
    const = lambda w0, w1: pl.BlockSpec((w0, w1), lambda i: (0, 0), pipeline_mode=pl.Buffered(1))
    tile = lambda w: pl.BlockSpec((tm, w), lambda i: (i, 0))
    return pl.pallas_call(
        functools.partial(_ffn_kernel, tiles_per_seq=seq // tm),
        grid=(n // tm,),
        in_specs=[
            tile(d), tile(A_WIDTH), pl.BlockSpec((HEAD_PAIRS, tm, LANES), lambda i: (0, i, 0)),
            const(d, d), const(1, d), const(1, d), const(d, ff), const(d, ff),
            const(CONV_WIDTH, ff), const(1, ff), const(ff, d), const(1, d),
        ],
        out_specs=tile(d),
        out_shape=jax.ShapeDtypeStruct((n, d), F32),
        scratch_shapes=[
            pltpu.VMEM((tm // ROWS_PER_PASS, halo + ROWS_PER_PASS, FF_CHUNK), F32),
            pltpu.VMEM((ff // FF_CHUNK, halo, FF_CHUNK), F32),
            pltpu.VMEM((tm, ff), BF16),
        ],
        compiler_params=pltpu.CompilerParams(
            dimension_semantics=("arbitrary",), vmem_limit_bytes=VMEM_LIMIT),
        name="outproj_ffn",
    )(x2, a, b, wo, gpost, gpre2, wg, wu, cw, cb, wd, gpost2)


def kernel(x, norm_mix_pre, norm_mix_post, norm_ffn_pre, norm_ffn_post, w_in, ln_v_gain, ln_v_bias,
           spatial_w, spatial_b, rel_bias, w_out, w_gate, w_up, conv_w, conv_b, w_down):
    batch, seq, d = x.shape
    depth = w_in.shape[0]
    ff = w_gate.shape[-1]
    assert seq % PROJ_TOKENS_PER_STEP == 0 and seq % FFN_TOKENS_PER_STEP == 0
    assert ff % FF_CHUNK == 0 and d == A_WIDTH + B_WIDTH
    assert seq // KEYS_BACK == max(DILATIONS)

    gmat = jnp.asarray(np.kron(np.eye(A_GROUPS), np.ones((HEAD_DIM, HEAD_DIM))), BF16)
    x2 = x.reshape(batch * seq, d)
    for l in range(depth):
        row = lambda p: p[l].reshape(1, -1).astype(F32)
        sb = jnp.repeat(spatial_b[l].astype(F32).T, HEAD_DIM, axis=1)
        a_out, q, k, v = _proj_call(
            x2, row(norm_mix_pre), w_in[l].astype(BF16), gmat,
            row(ln_v_gain), row(ln_v_bias), spatial_w[l].astype(F32), sb)
        b_out = _attn_call(rel_bias.astype(F32), q, k, v, batch, seq)
        x2 = _ffn_call(
            x2, a_out, b_out, w_out[l].astype(BF16), row(norm_mix_post), row(norm_ffn_pre),
            w_gate[l].astype(BF16), w_up[l].astype(BF16), conv_w[l].astype(F32), row(conv_b),
            w_down[l].astype(BF16), row(norm_ffn_post), seq)
    return x2.reshape(batch, seq, d)
```

```python
import functools
import math

import numpy as np
import jax
import jax.numpy as jnp
from jax import lax
from jax.experimental import pallas as pl
from jax.experimental.pallas import tpu as pltpu

F32 = jnp.float32
BF16 = jnp.bfloat16

HEAD_DIM = 64
A_GROUPS = 4
A_WIDTH = A_GROUPS * HEAD_DIM
B_HEADS = 12
B_WIDTH = B_HEADS * HEAD_DIM
CHUNK = 128
DILATIONS = (1, 4, 16)
KEYS_BACK = 128
NUM_BUCKETS = 32
MAX_DISTANCE = 2048
CONV_WIDTH = 3
NORM_EPS = 1e-6
NEG_INF = -1e30
LOG2E = 1.4426950408889634

LANES = 128
HEADS_PER_STEP = LANES // HEAD_DIM
HEAD_PAIRS = B_HEADS // HEADS_PER_STEP
FF_CHUNK = 256
PROJ_TOKENS_PER_STEP = 1024
FFN_TOKENS_PER_STEP = 512
FFN_TAIL_ROWS = 256
ROWS_PER_PASS = 512
VMEM_LIMIT = 56 * 1024 * 1024


def _rms(x, g):
    ms = jnp.mean(x * x, axis=-1, keepdims=True)
    return x * lax.rsqrt(ms + NORM_EPS) * g


def _dot(a, b):
    return jnp.dot(a, b, preferred_element_type=F32)


def _dot_nt(a, b):
    return lax.dot_general(a, b, (((1,), (1,)), ((), ())), preferred_element_type=F32)


def _proj_kernel(x_ref, g_ref, w_ref, gmat_ref, lng_ref, lnb_ref, sw_ref, sb_ref,
                 a_ref, q_ref, k_ref, v_ref):
    gmat = gmat_ref[...]
    row = lax.broadcasted_iota(jnp.int32, (CHUNK, CHUNK), 0)
    col = lax.broadcasted_iota(jnp.int32, (CHUNK, CHUNK), 1)
    tril = row >= col
    ws = [jnp.where(tril, sw_ref[g], 0.0).astype(BF16) for g in range(A_GROUPS)]
    grp = lax.broadcasted_iota(jnp.int32, (CHUNK, A_WIDTH), 1) // HEAD_DIM
    sb = sb_ref[...]
    o = 2 * A_WIDTH

    def group_sum(t):
        hi = t.astype(BF16)
        lo = (t - hi.astype(F32)).astype(BF16)
        return _dot(hi, gmat) + _dot(lo, gmat)

    for p in range(x_ref.shape[0] // ROWS_PER_PASS):
        rows = pl.ds(p * ROWS_PER_PASS, ROWS_PER_PASS)
        h = _rms(x_ref[rows, :], g_ref[...]).astype(BF16)

        for n, ref in enumerate((q_ref, k_ref, v_ref)):
            val = _dot(h, w_ref[:, o + n * B_WIDTH:o + (n + 1) * B_WIDTH])
            if n == 0:
                val = val * (LOG2E / math.sqrt(HEAD_DIM))
            for hp in range(HEAD_PAIRS):
                ref[hp, rows, :] = val[:, hp * LANES:(hp + 1) * LANES]

        uv = _dot(h, w_ref[:, 0:o])
        u = jax.nn.gelu(uv[:, :A_WIDTH])
        vf = jax.nn.gelu(uv[:, A_WIDTH:])
        mu = group_sum(vf) * (1.0 / HEAD_DIM)
        dl = vf - mu
        var = group_sum(dl * dl) * (1.0 / HEAD_DIM)
        vn = (dl * lax.rsqrt(var + NORM_EPS) * lng_ref[...] + lnb_ref[...]).astype(BF16)

        for c in range(ROWS_PER_PASS // CHUNK):
            chunk = slice(c * CHUNK, (c + 1) * CHUNK)
            vc = vn[chunk]
            z = _dot(ws[0], vc)
            for g in range(1, A_GROUPS):
                z = jnp.where(grp == g, _dot(ws[g], vc), z)
            a_ref[pl.ds(p * ROWS_PER_PASS + c * CHUNK, CHUNK), :] = (u[chunk] * (z + sb)).astype(BF16)


def _proj_call(x2, g_pre, w_in, gmat, lng, lnb, sw, sb):
    n, d = x2.shape
    tm = PROJ_TOKENS_PER_STEP
    cols = w_in.shape[1]
    const = lambda i: (0, 0)
    once = pl.Buffered(1)
    return pl.pallas_call(
        _proj_kernel,
        grid=(n // tm,),
        in_specs=[
            pl.BlockSpec((tm, d), lambda i: (i, 0)),
            pl.BlockSpec((1, d), const, pipeline_mode=once),
            pl.BlockSpec((d, cols), const, pipeline_mode=once),
            pl.BlockSpec((A_WIDTH, A_WIDTH), const, pipeline_mode=once),
            pl.BlockSpec((1, A_WIDTH), const, pipeline_mode=once),
            pl.BlockSpec((1, A_WIDTH), const, pipeline_mode=once),
            pl.BlockSpec((A_GROUPS, CHUNK, CHUNK), lambda i: (0, 0, 0), pipeline_mode=once),
            pl.BlockSpec((CHUNK, A_WIDTH), const, pipeline_mode=once),
        ],
        out_specs=[
            pl.BlockSpec((tm, A_WIDTH), lambda i: (i, 0)),
            pl.BlockSpec((HEAD_PAIRS, tm, LANES), lambda i: (0, i, 0)),
            pl.BlockSpec((HEAD_PAIRS, tm, LANES), lambda i: (0, i, 0)),
            pl.BlockSpec((HEAD_PAIRS, tm, LANES), lambda i: (0, i, 0)),
        ],
        out_shape=[
            jax.ShapeDtypeStruct((n, A_WIDTH), BF16),
            jax.ShapeDtypeStruct((HEAD_PAIRS, n, LANES), F32),
            jax.ShapeDtypeStruct((HEAD_PAIRS, n, LANES), F32),
            jax.ShapeDtypeStruct((HEAD_PAIRS, n, LANES), F32),
        ],
        compiler_params=pltpu.CompilerParams(
            dimension_semantics=("arbitrary",), vmem_limit_bytes=VMEM_LIMIT),
        name="proj_gmlp",
    )(x2, g_pre, w_in, gmat, lng, lnb, sw, sb)


def _t5_bucket_np(dist):
    max_exact = NUM_BUCKETS // 2
    d = np.maximum(dist, 1).astype(np.float64)
    large = max_exact + (np.log(d / max_exact) / math.log(MAX_DISTANCE / max_exact)
                         * (NUM_BUCKETS - max_exact))
    large = np.minimum(large.astype(np.int32), NUM_BUCKETS - 1)
    return np.where(dist < max_exact, dist, large)


def _bucket_starts(d):
    b = _t5_bucket_np(np.arange(KEYS_BACK + 1) * d)
    return [(r, int(b[r])) for r in range(KEYS_BACK + 1) if r == 0 or b[r] != b[r - 1]]


def _attn_kernel(rb_ref, q_ref, k_ref, v_ref, o_ref, qs, kp, va, bt, stg, mb, lb, ab):
    n_res = DILATIONS[1]
    seq = q_ref.shape[0]
    nblk = seq // KEYS_BACK
    n_blocks = len(DILATIONS) * nblk
    quarter = seq // n_res
    assert DILATIONS == (1, 4, 16) and nblk == DILATIONS[-1]
    lane = lax.broadcasted_iota(jnp.int32, (KEYS_BACK, LANES), 1)
    first_head = lane < HEAD_DIM

    @pl.when(pl.program_id(1) == 0)
    def _():
        i = lax.broadcasted_iota(jnp.int32, (KEYS_BACK, 2 * KEYS_BACK), 0)
        j = lax.broadcasted_iota(jnp.int32, (KEYS_BACK, 2 * KEYS_BACK), 1)
        for di, d in enumerate(DILATIONS):
            for hh in range(HEADS_PER_STEP):
                head = pl.program_id(0) * HEADS_PER_STEP + hh
                rows = pl.ds(hh * KEYS_BACK, KEYS_BACK)
                for variant, rel in enumerate((KEYS_BACK + i - j, i - j)):
                    val = jnp.zeros(rel.shape, F32)
                    for start, bucket in _bucket_starts(d):
                        val = jnp.where(rel >= start, rb_ref[bucket, head], val)
                    band = (rel >= 0) & (rel <= KEYS_BACK)
                    bt[2 * di + variant, rows, :] = jnp.where(band, val * LOG2E, NEG_INF)

    def block_rows(j):
        return pl.ds(j * KEYS_BACK, KEYS_BACK)

    @pl.when((pl.program_id(0) == 0) & (pl.program_id(1) == 0))
    def _():
        kp[...] = jnp.zeros(kp.shape, BF16)
        rows_total = va.shape[1]
        lane_all = lax.broadcasted_iota(jnp.int32, (rows_total, LANES), 1)
        for hh in range(HEADS_PER_STEP):
            own = (lane_all // HEAD_DIM) == hh
            va[hh, :, 0:LANES] = jnp.zeros((rows_total, LANES), BF16)
            va[hh, :, LANES:] = jnp.where(own, 1.0, 0.0).astype(BF16)

    def regroup(j):
        di, g = j // nblk, j % nblk
        if di == 0:
            qv, kv, vv = (ref[block_rows(g), :] for ref in (q_ref, k_ref, v_ref))
        elif di == 1:
            tok = pl.ds(g // n_res + (g % n_res) * KEYS_BACK * n_res, KEYS_BACK, stride=n_res)
            qv, kv, vv = (ref[tok, :] for ref in (q_ref, k_ref, v_ref))
            for n, val in enumerate((qv, kv, vv)):
                stg[n, block_rows(g), :] = val
        else:
            rows = pl.ds((g % n_res) * quarter + g // n_res, KEYS_BACK, stride=n_res)
            qv, kv, vv = (stg[n, rows, :] for n in range(3))
        qs[j, 0:KEYS_BACK, :] = jnp.where(first_head, qv, 0.0).astype(BF16)
        qs[j, KEYS_BACK:, :] = jnp.where(first_head, 0.0, qv).astype(BF16)
        kp[block_rows(j), :] = kv.astype(BF16)
        va[0, block_rows(j), 0:LANES] = jnp.where(first_head, vv, 0.0).astype(BF16)
        va[1, block_rows(j), 0:LANES] = jnp.where(first_head, 0.0, vv).astype(BF16)

    def out_rows(j):
        if j // nblk == 2:
            r = j % nblk
            return pl.ds(2 * seq + (r % n_res) * quarter + r // n_res, KEYS_BACK, stride=n_res)
        return block_rows(j)

    def attend(j):
        di, g = j // nblk, j % nblk
        no_prev = g % (nblk // DILATIONS[di]) == 0
        keys = pl.ds((j if no_prev else j - 1) * KEYS_BACK, 2 * KEYS_BACK)
        s = _dot_nt(qs[j], kp[keys, :]) + bt[2 * di + int(no_prev)]
        m = jnp.max(s, axis=-1, keepdims=True)
        mb[out_rows(j), :] = jnp.where(first_head, m[:KEYS_BACK], m[KEYS_BACK:])
        p = jnp.exp2(s - m).astype(BF16)
        p = jnp.concatenate([p[:KEYS_BACK], p[KEYS_BACK:]], axis=1)
        vals = jnp.concatenate([va[0, keys, :], va[1, keys, :]], axis=0)
        r = _dot(p, vals)
        ab[out_rows(j), :] = r[:, :LANES]
        lb[out_rows(j), :] = r[:, LANES:]

    def combine(c):
        tok = pl.ds(c // n_res + (c % n_res) * KEYS_BACK * n_res, KEYS_BACK, stride=n_res)
        v0, v1, v2 = tok, block_rows(nblk + c), block_rows(2 * nblk + c)
        ms = [mb[v, :] for v in (v0, v1, v2)]
        mx = jnp.maximum(jnp.maximum(ms[0], ms[1]), ms[2])
        e0, e1, e2 = (jnp.exp2(m - mx) for m in ms)
        num = e0 * ab[v0, :] + e1 * ab[v1, :] + e2 * ab[v2, :]
        den = e0 * lb[v0, :] + e1 * lb[v1, :] + e2 * lb[v2, :]
        o_ref[tok, :] = num / den

    for j in range(nblk, 2 * nblk):
        regroup(j)
    for j in list(range(nblk)) + list(range(2 * nblk, n_blocks)):
        regroup(j)
    for j in range(n_blocks):
        attend(j)
    for c in range(nblk):
        combine(c)


def _attn_call(rel_bias, q, k, v, batch, seq):
    nd = len(DILATIONS)
    nblk = seq // KEYS_BACK
    io_spec = pl.BlockSpec((None, seq, LANES), lambda hp, b: (hp, b, 0))
    return pl.pallas_call(
        _attn_kernel,
        grid=(HEAD_PAIRS, batch),
        in_specs=[pl.BlockSpec(memory_space=pltpu.SMEM), io_spec, io_spec, io_spec],
        out_specs=io_spec,
        out_shape=jax.ShapeDtypeStruct((HEAD_PAIRS, batch * seq, LANES), F32),
        scratch_shapes=[
            pltpu.VMEM((nd * nblk, 2 * KEYS_BACK, LANES), BF16),
            pltpu.VMEM((nd * seq + KEYS_BACK, LANES), BF16),
            pltpu.VMEM((HEADS_PER_STEP, nd * seq + KEYS_BACK, 2 * LANES), BF16),
            pltpu.VMEM((2 * nd, 2 * KEYS_BACK, 2 * KEYS_BACK), F32),
            pltpu.VMEM((3, seq, LANES), F32),
            pltpu.VMEM((nd * seq, LANES), F32),
            pltpu.VMEM((nd * seq, LANES), F32),
            pltpu.VMEM((nd * seq, LANES), F32),
        ],
        compiler_params=pltpu.CompilerParams(
            dimension_semantics=("arbitrary", "arbitrary"), vmem_limit_bytes=VMEM_LIMIT),
        name="dilated_attn",
    )(rel_bias, q, k, v)


def _ffn_kernel(xn_ref, an_ref, bn_ref, x0_ref, a0_ref, b0_ref, wo_ref, gpost_ref, gpre2_ref,
                wg_ref, wu_ref, cw_ref, cb_ref, wd_ref, gpost2_ref, o_ref,
                gbuf, carry, act_buf, x1s, h2s, *, tiles_per_seq):
    tm = o_ref.shape[0]
    halo = carry.shape[1]
    n_chunks = wd_ref.shape[0] // FF_CHUNK
    step = pl.program_id(0)
    slot = step % 2

    row_blocks = [pl.ds(r * FFN_TAIL_ROWS, FFN_TAIL_ROWS) for r in range(tm // FFN_TAIL_ROWS)]

    def mix_proj(a_ref, b_ref):
        b = jnp.concatenate([b_ref[hp] for hp in range(HEAD_PAIRS)], axis=1)
        return _dot(a_ref[...], wo_ref[0:A_WIDTH, :]) + _dot(b.astype(BF16), wo_ref[A_WIDTH:, :])

    def mix_norms(x_ref, y1, rows, dst):
        x1 = x_ref[rows, :] + _rms(y1[rows.start:rows.start + rows.size], gpost_ref[...])
        x1s[dst, rows, :] = x1
        h2s[dst, rows, :] = _rms(x1, gpre2_ref[...]).astype(BF16)

    @pl.when(step == 0)
    def _():
        y1 = mix_proj(a0_ref, b0_ref)
        for rows in row_blocks:
            mix_norms(x0_ref, y1, rows, 0)

    @pl.when(step % tiles_per_seq == 0)
    def _():
        carry[...] = jnp.zeros(carry.shape, F32)

    h2 = h2s[slot]

    def gate_up(c):
        cols = slice(c * FF_CHUNK, (c + 1) * FF_CHUNK)
        return _dot(h2, wg_ref[:, cols]), _dot(h2, wu_ref[:, cols])

    gu = gate_up(0)
    for c in range(n_chunks):
        cols = slice(c * FF_CHUNK, (c + 1) * FF_CHUNK)
        gu_next = gate_up(c + 1) if c + 1 < n_chunks else None
        g, up = gu
        gbuf[0:halo, :] = carry[c]
        gbuf[halo:halo + tm, :] = g
        carry[c] = g[tm - halo:, :]
        cw = cw_ref[:, cols]
        cv = (cb_ref[:, cols]
              + gbuf[halo - 2:halo - 2 + tm, :] * cw[0:1]
              + gbuf[halo - 1:halo - 1 + tm, :] * cw[1:2]
              + g * cw[2:3])
        act_buf[:, cols] = (jax.nn.gelu(cv) * up).astype(BF16)
        gu = gu_next

    y1_next = mix_proj(an_ref, bn_ref)
    down = []
    for rows in row_blocks:
        down.append(_dot(act_buf[rows, :], wd_ref[...]))
        mix_norms(xn_ref, y1_next, rows, 1 - slot)
        if len(down) > 1:
            prev = row_blocks[len(down) - 2]
            o_ref[prev, :] = x1s[slot, prev, :] + _rms(down[-2], gpost2_ref[...])
    o_ref[row_blocks[-1], :] = x1s[slot, row_blocks[-1], :] + _rms(down[-1], gpost2_ref[...])


def _ffn_call(x2, a, b, wo, gpost, gpre2, wg, wu, cw, cb, wd, gpost2, seq):
    n, d = x2.shape
    tm = FFN_TOKENS_PER_STEP
    n_tiles = n // tm
    ff = wd.shape[0]
    halo = 8
    once = pl.Buffered(1)
    const = lambda w0, w1: pl.BlockSpec((w0, w1), lambda i: (0, 0), pipeline_mode=once)
    nxt = lambda i: jnp.minimum(i + 1, n_tiles - 1)
    return pl.pallas_call(
        functools.partial(_ffn_kernel, tiles_per_seq=seq // tm),
        grid=(n_tiles,),
        in_specs=[
            pl.BlockSpec((tm, d), lambda i: (nxt(i), 0)),
            pl.BlockSpec((tm, A_WIDTH), lambda i: (nxt(i), 0)),
            pl.BlockSpec((HEAD_PAIRS, tm, LANES), lambda i: (0, nxt(i), 0)),
            pl.BlockSpec((tm, d), lambda i: (0, 0), pipeline_mode=once),
            pl.BlockSpec((tm, A_WIDTH), lambda i: (0, 0), pipeline_mode=once),
            pl.BlockSpec((HEAD_PAIRS, tm, LANES), lambda i: (0, 0, 0), pipeline_mode=once),
            const(d, d), const(1, d), const(1, d), const(d, ff), const(d, ff),
            const(CONV_WIDTH, ff), const(1, ff), const(ff, d), const(1, d),
        ],
        out_specs=pl.BlockSpec((tm, d), lambda i: (i, 0)),
        out_shape=jax.ShapeDtypeStruct((n, d), F32),
        scratch_shapes=[
            pltpu.VMEM((halo + tm, FF_CHUNK), F32),
            pltpu.VMEM((ff // FF_CHUNK, halo, FF_CHUNK), F32),
            pltpu.VMEM((tm, ff), BF16),
            pltpu.VMEM((2, tm, d), F32),
            pltpu.VMEM((2, tm, d), BF16),
        ],
        compiler_params=pltpu.CompilerParams(
            dimension_semantics=("arbitrary",), vmem_limit_bytes=VMEM_LIMIT),
        name="outproj_ffn",
    )(x2, a, b, x2, a, b, wo, gpost, gpre2, wg, wu, cw, cb, wd, gpost2)


def kernel(x, norm_mix_pre, norm_mix_post, norm_ffn_pre, norm_ffn_post, w_in, ln_v_gain, ln_v_bias,
           spatial_w, spatial_b, rel_bias, w_out, w_gate, w_up, conv_w, conv_b, w_down):
    batch, seq, d = x.shape
    depth = w_in.shape[0]
    ff = w_gate.shape[-1]
    assert seq % PROJ_TOKENS_PER_STEP == 0 and seq % FFN_TOKENS_PER_STEP == 0
    assert ff % FF_CHUNK == 0 and d == A_WIDTH + B_WIDTH
    assert seq // KEYS_BACK == max(DILATIONS)

    gmat = jnp.asarray(np.kron(np.eye(A_GROUPS), np.ones((HEAD_DIM, HEAD_DIM))), BF16)
    x2 = x.reshape(batch * seq, d)
    for l in range(depth):
        row = lambda p: p[l].reshape(1, -1).astype(F32)
        sb = jnp.repeat(spatial_b[l].astype(F32).T, HEAD_DIM, axis=1)
        a_out, q, k, v = _proj_call(
            x2, row(norm_mix_pre), w_in[l].astype(BF16), gmat,
            row(ln_v_gain), row(ln_v_bias), spatial_w[l].astype(F32), sb)
        b_out = _attn_call(rel_bias.astype(F32), q, k, v, batch, seq)
        x2 = _ffn_call(
            x2, a_out, b_out, w_out[l].astype(BF16), row(norm_mix_post), row(norm_ffn_pre),
            w_gate[l].astype(BF16), w_up[l].astype(BF16), conv_w[l].astype(F32), row(conv_b),
            w_down[l].astype(BF16), row(norm_ffn_post), seq)
    return x2.reshape(batch, seq, d)
```

```python
import functools
import math

import numpy as np
import jax
import jax.numpy as jnp
from jax import lax
from jax.experimental import pallas as pl
from jax.experimental.pallas import tpu as pltpu

F32 = jnp.float32
BF16 = jnp.bfloat16

HEAD_DIM = 64
A_GROUPS = 4
A_WIDTH = A_GROUPS * HEAD_DIM
B_HEADS = 12
B_WIDTH = B_HEADS * HEAD_DIM
CHUNK = 128
DILATIONS = (1, 4, 16)
KEYS_BACK = 128
NUM_BUCKETS = 32
MAX_DISTANCE = 2048
CONV_WIDTH = 3
NORM_EPS = 1e-6
NEG_INF = -1e30
LOG2E = 1.4426950408889634

LANES = 128
HEADS_PER_STEP = LANES // HEAD_DIM
FF_CHUNK = 256
PROJ_TOKENS_PER_STEP = 1024
ROWS_PER_PASS = 512
FFN_TOKENS_PER_STEP = 512
VMEM_LIMIT = 56 * 1024 * 1024


def _rms(x, g):
    ms = jnp.mean(x * x, axis=-1, keepdims=True)
    return x * lax.rsqrt(ms + NORM_EPS) * g


def _dot(a, b):
    return jnp.dot(a, b, preferred_element_type=F32)


def _dot_nt(a, b):
    return lax.dot_general(a, b, (((1,), (1,)), ((), ())), preferred_element_type=F32)


def _proj_kernel(x_ref, g_ref, w_ref, gmat_ref, lng_ref, lnb_ref, sw_ref, sb_ref,
                 a_ref, q_ref, k_ref, v_ref, uv_ref):
    gmat = gmat_ref[...]
    row = lax.broadcasted_iota(jnp.int32, (CHUNK, CHUNK), 0)
    col = lax.broadcasted_iota(jnp.int32, (CHUNK, CHUNK), 1)
    tril = row >= col
    ws = [jnp.where(tril, sw_ref[g], 0.0).astype(BF16) for g in range(A_GROUPS)]
    grp = lax.broadcasted_iota(jnp.int32, (CHUNK, A_WIDTH), 1) // HEAD_DIM
    sb = sb_ref[...]
    o = 2 * A_WIDTH

    def group_sum(t):
        hi = t.astype(BF16)
        lo = (t - hi.astype(F32)).astype(BF16)
        return _dot(hi, gmat) + _dot(lo, gmat)

    n_pass = x_ref.shape[0] // ROWS_PER_PASS
    pass_rows = [pl.ds(p * ROWS_PER_PASS, ROWS_PER_PASS) for p in range(n_pass)]

    for rows in pass_rows:
        h = _rms(x_ref[rows, :], g_ref[...]).astype(BF16)
        q_ref[rows, :] = _dot(h, w_ref[:, o:o + B_WIDTH]) * (LOG2E / math.sqrt(HEAD_DIM))
        k_ref[rows, :] = _dot(h, w_ref[:, o + B_WIDTH:o + 2 * B_WIDTH])
        v_ref[rows, :] = _dot(h, w_ref[:, o + 2 * B_WIDTH:o + 3 * B_WIDTH])
        uv_ref[rows, :] = _dot(h, w_ref[:, 0:o])

    for rows in pass_rows:
        u = jax.nn.gelu(uv_ref[rows, 0:A_WIDTH])
        vf = jax.nn.gelu(uv_ref[rows, A_WIDTH:o])
        mu = group_sum(vf) * (1.0 / HEAD_DIM)
        dl = vf - mu
        var = group_sum(dl * dl) * (1.0 / HEAD_DIM)
        vn = (dl * lax.rsqrt(var + NORM_EPS) * lng_ref[...] + lnb_ref[...]).astype(BF16)

        for c in range(ROWS_PER_PASS // CHUNK):
            chunk = slice(c * CHUNK, (c + 1) * CHUNK)
            vc = vn[chunk]
            z = _dot(ws[0], vc)
            for g in range(1, A_GROUPS):
                z = jnp.where(grp == g, _dot(ws[g], vc), z)
            a_ref[pl.ds(rows.start + c * CHUNK, CHUNK), :] = (u[chunk] * (z + sb)).astype(BF16)


def _proj_call(x2, g_pre, w_in, gmat, lng, lnb, sw, sb):
    n, d = x2.shape
    tm = PROJ_TOKENS_PER_STEP
    cols = w_in.shape[1]
    const = lambda i: (0, 0)
    once = pl.Buffered(1)
    return pl.pallas_call(
        _proj_kernel,
        grid=(n // tm,),
        in_specs=[
            pl.BlockSpec((tm, d), lambda i: (i, 0)),
            pl.BlockSpec((1, d), const, pipeline_mode=once),
            pl.BlockSpec((d, cols), const, pipeline_mode=once),
            pl.BlockSpec((A_WIDTH, A_WIDTH), const, pipeline_mode=once),
            pl.BlockSpec((1, A_WIDTH), const, pipeline_mode=once),
            pl.BlockSpec((1, A_WIDTH), const, pipeline_mode=once),
            pl.BlockSpec((A_GROUPS, CHUNK, CHUNK), lambda i: (0, 0, 0), pipeline_mode=once),
            pl.BlockSpec((CHUNK, A_WIDTH), const, pipeline_mode=once),
        ],
        out_specs=[
            pl.BlockSpec((tm, A_WIDTH), lambda i: (i, 0)),
            pl.BlockSpec((tm, B_WIDTH), lambda i: (i, 0)),
            pl.BlockSpec((tm, B_WIDTH), lambda i: (i, 0)),
            pl.BlockSpec((tm, B_WIDTH), lambda i: (i, 0)),
        ],
        out_shape=[
            jax.ShapeDtypeStruct((n, A_WIDTH), BF16),
            jax.ShapeDtypeStruct((n, B_WIDTH), F32),
            jax.ShapeDtypeStruct((n, B_WIDTH), F32),
            jax.ShapeDtypeStruct((n, B_WIDTH), F32),
        ],
        scratch_shapes=[pltpu.VMEM((tm, 2 * A_WIDTH), F32)],
        compiler_params=pltpu.CompilerParams(
            dimension_semantics=("arbitrary",), vmem_limit_bytes=VMEM_LIMIT),
        name="proj_gmlp",
    )(x2, g_pre, w_in, gmat, lng, lnb, sw, sb)


def _t5_bucket_np(dist):
    max_exact = NUM_BUCKETS // 2
    d = np.maximum(dist, 1).astype(np.float64)
    large = max_exact + (np.log(d / max_exact) / math.log(MAX_DISTANCE / max_exact)
                         * (NUM_BUCKETS - max_exact))
    large = np.minimum(large.astype(np.int32), NUM_BUCKETS - 1)
    return np.where(dist < max_exact, dist, large)


def _bucket_starts(d):
    b = _t5_bucket_np(np.arange(KEYS_BACK + 1) * d)
    return [(r, int(b[r])) for r in range(KEYS_BACK + 1) if r == 0 or b[r] != b[r - 1]]


def _attn_kernel(rb_ref, q_ref, k_ref, v_ref, o_ref,
                 qs, kp, va, bt, stg, mb, lb, ab):
    seq = q_ref.shape[0]
    nblk = seq // KEYS_BACK
    padded = seq + KEYS_BACK
    assert DILATIONS == (1, 4, 16) and nblk == DILATIONS[-1]
    lane = lax.broadcasted_iota(jnp.int32, (KEYS_BACK, LANES), 1)
    first_head = lane < HEAD_DIM

    @pl.when(pl.program_id(1) == 0)
    def _():
        i = lax.broadcasted_iota(jnp.int32, (KEYS_BACK, 2 * KEYS_BACK), 0)
        j = lax.broadcasted_iota(jnp.int32, (KEYS_BACK, 2 * KEYS_BACK), 1)
        rel = KEYS_BACK + i - j
        band = (rel >= 0) & (rel <= KEYS_BACK)
        band_cur = band & (j >= KEYS_BACK)
        for di, d in enumerate(DILATIONS):
            for hh in range(HEADS_PER_STEP):
                head = pl.program_id(0) * HEADS_PER_STEP + hh
                val = jnp.zeros(rel.shape, F32)
                for start, bucket in _bucket_starts(d):
                    val = jnp.where(rel >= start, rb_ref[bucket, head], val)
                val = val * LOG2E
                rows = pl.ds(hh * KEYS_BACK, KEYS_BACK)
                bt[2 * di, rows, :] = jnp.where(band, val, NEG_INF)
                bt[2 * di + 1, rows, :] = jnp.where(band_cur, val, NEG_INF)

    quarter = seq // 4

    def d4_rows(c):
        return pl.ds(c // 4 + (c % 4) * KEYS_BACK * 4, KEYS_BACK, stride=4)

    def d16_rows_in_d4_order(r):
        return pl.ds((r % 4) * quarter + r // 4, KEYS_BACK, stride=4)

    def put_block(j, key_row, qv, kv, vv):
        qs[j, 0:KEYS_BACK, :] = jnp.where(first_head, qv, 0.0).astype(BF16)
        qs[j, KEYS_BACK:, :] = jnp.where(first_head, 0.0, qv).astype(BF16)
        kp[pl.ds(key_row, KEYS_BACK), :] = kv.astype(BF16)
        va[pl.ds(key_row, KEYS_BACK), 0:LANES] = vv.astype(BF16)

    @pl.when((pl.program_id(0) == 0) & (pl.program_id(1) == 0))
    def _():
        for di in range(len(DILATIONS)):
            kp[pl.ds(di * padded, KEYS_BACK), :] = jnp.zeros((KEYS_BACK, LANES), BF16)
            va[pl.ds(di * padded, KEYS_BACK), :] = jnp.zeros((KEYS_BACK, 2 * LANES), BF16)
            va[pl.ds(di * padded + KEYS_BACK, seq), LANES:] = jnp.ones((seq, LANES), BF16)

    for g in range(nblk):
        rows = pl.ds(g * KEYS_BACK, KEYS_BACK)
        put_block(g, (g + 1) * KEYS_BACK, q_ref[rows, :], k_ref[rows, :], v_ref[rows, :])
    for g in range(nblk):
        src = d4_rows(g)
        rows = pl.ds(g * KEYS_BACK, KEYS_BACK)
        qv, kv, vv = q_ref[src, :], k_ref[src, :], v_ref[src, :]
        stg[0, rows, :] = qv
        stg[1, rows, :] = kv
        stg[2, rows, :] = vv
        put_block(nblk + g, padded + (g + 1) * KEYS_BACK, qv, kv, vv)
    for g in range(nblk):
        src = d16_rows_in_d4_order(g)
        put_block(2 * nblk + g, 2 * padded + (g + 1) * KEYS_BACK,
                  stg[0, src, :], stg[1, src, :], stg[2, src, :])

    def out_rows(j):
        if j // nblk == 2:
            r = j % nblk
            return pl.ds(2 * seq + (r % 4) * quarter + r // 4, KEYS_BACK, stride=4)
        return pl.ds(j * KEYS_BACK, KEYS_BACK)

    def key_rows(j):
        return pl.ds((j // nblk) * padded + (j % nblk) * KEYS_BACK, 2 * KEYS_BACK)

    def qk(j):
        return _dot_nt(qs[j], kp[key_rows(j), :])

    def softmax(j, s):
        di, g = j // nblk, j % nblk
        no_prev = g % (nblk // DILATIONS[di]) == 0
        s = s + bt[2 * di + int(no_prev)]
        m = jnp.max(s, axis=-1, keepdims=True)
        mb[out_rows(j), :] = jnp.where(first_head, m[:KEYS_BACK], m[KEYS_BACK:])
        return jnp.exp2(s - m).astype(BF16)

    def pv(j, p):
        r = _dot(p, va[key_rows(j), :])
        ab[out_rows(j), :] = jnp.where(first_head, r[:KEYS_BACK, :LANES], r[KEYS_BACK:, :LANES])
        lb[out_rows(j), :] = jnp.where(first_head, r[:KEYS_BACK, LANES:], r[KEYS_BACK:, LANES:])

    n_blocks = len(DILATIONS) * nblk
    for j in range(n_blocks):
        pv(j, softmax(j, qk(j)))

    for c in range(nblk):
        tok = d4_rows(c)
        views = (tok, pl.ds(seq + c * KEYS_BACK, KEYS_BACK), pl.ds(2 * seq + c * KEYS_BACK, KEYS_BACK))
        ms = [mb[v, :] for v in views]
        mx = jnp.maximum(jnp.maximum(ms[0], ms[1]), ms[2])
        num = jnp.zeros((KEYS_BACK, LANES), F32)
        den = jnp.zeros((KEYS_BACK, LANES), F32)
        for m, v in zip(ms, views):
            e = jnp.exp2(m - mx)
            num = num + e * ab[v, :]
            den = den + e * lb[v, :]
        o_ref[tok, :] = num / den


def _attn_call(rel_bias, q, k, v, batch, seq):
    nd = len(DILATIONS)
    nblk = seq // KEYS_BACK
    n_pairs = B_HEADS // HEADS_PER_STEP
    io_spec = pl.BlockSpec((seq, LANES), lambda hp, b: (b, hp))
    return pl.pallas_call(
        _attn_kernel,
        grid=(n_pairs, batch),
        in_specs=[
            pl.BlockSpec(memory_space=pltpu.SMEM),
            io_spec, io_spec, io_spec,
        ],
        out_specs=io_spec,
        out_shape=jax.ShapeDtypeStruct((batch * seq, B_WIDTH), F32),
        scratch_shapes=[
            pltpu.VMEM((nd * nblk, 2 * KEYS_BACK, LANES), BF16),
            pltpu.VMEM((nd * (seq + KEYS_BACK), LANES), BF16),
            pltpu.VMEM((nd * (seq + KEYS_BACK), 2 * LANES), BF16),
            pltpu.VMEM((2 * nd, 2 * KEYS_BACK, 2 * KEYS_BACK), F32),
            pltpu.VMEM((3, seq, LANES), F32),
            pltpu.VMEM((nd * seq, LANES), F32),
            pltpu.VMEM((nd * seq, LANES), F32),
            pltpu.VMEM((nd * seq, LANES), F32),
        ],
        compiler_params=pltpu.CompilerParams(
            dimension_semantics=("arbitrary", "arbitrary"), vmem_limit_bytes=VMEM_LIMIT),
        name="dilated_attn",
    )(rel_bias, q, k, v)


def _ffn_kernel(x_ref, a_ref, b_ref, wo_ref, gpost_ref, gpre2_ref, wg_ref, wu_ref, cw_ref, cb_ref,
                wd_ref, gpost2_ref, o_ref, gbuf, carry, act_buf, *, tiles_per_seq):
    tm = x_ref.shape[0]
    halo = carry.shape[1]
    n_chunks = wd_ref.shape[0] // FF_CHUNK
    first = pl.program_id(0) % tiles_per_seq == 0

    @pl.when(first)
    def _():
        carry[...] = jnp.zeros(carry.shape, F32)

    y1 = (_dot(a_ref[...], wo_ref[0:A_WIDTH, :])
          + _dot(b_ref[...].astype(BF16), wo_ref[A_WIDTH:, :]))
    x1 = x_ref[...] + _rms(y1, gpost_ref[...])
    h2 = _rms(x1, gpre2_ref[...]).astype(BF16)

    def gate_up(c):
        cols = slice(c * FF_CHUNK, (c + 1) * FF_CHUNK)
        return _dot(h2, wg_ref[:, cols]), _dot(h2, wu_ref[:, cols])

    gu = gate_up(0)
    for c in range(n_chunks):
        cols = slice(c * FF_CHUNK, (c + 1) * FF_CHUNK)
        gu_next = gate_up(c + 1) if c + 1 < n_chunks else None
        g, up = gu
        gbuf[0:halo, :] = carry[c]
        gbuf[halo:halo + tm, :] = g
        carry[c] = g[tm - halo:, :]
        cw = cw_ref[:, cols]
        cv = (cb_ref[:, cols]
              + gbuf[halo - 2:halo - 2 + tm, :] * cw[0:1]
              + gbuf[halo - 1:halo - 1 + tm, :] * cw[1:2]
              + g * cw[2:3])
        act_buf[:, cols] = (jax.nn.gelu(cv) * up).astype(BF16)
        gu = gu_next
    o_ref[...] = x1 + _rms(_dot(act_buf[...], wd_ref[...]), gpost2_ref[...])


def _ffn_call(x2, a, b, wo, gpost, gpre2, wg, wu, cw, cb, wd, gpost2, seq):
    n, d = x2.shape
    tm = FFN_TOKENS_PER_STEP
    ff = wd.shape[0]
    halo = 8
    const = lambda i: (0, 0)
    tile = lambda w: pl.BlockSpec((tm, w), lambda i: (i, 0))
    return pl.pallas_call(
        functools.partial(_ffn_kernel, tiles_per_seq=seq // tm),
        grid=(n // tm,),
        in_specs=[
            tile(d), tile(A_WIDTH), tile(B_WIDTH),
            pl.BlockSpec((d, d), const),
            pl.BlockSpec((1, d), const),
            pl.BlockSpec((1, d), const),
            pl.BlockSpec((d, ff), const),
            pl.BlockSpec((d, ff), const),
            pl.BlockSpec((CONV_WIDTH, ff), const),
            pl.BlockSpec((1, ff), const),
            pl.BlockSpec((ff, d), const),
            pl.BlockSpec((1, d), const),
        ],
        out_specs=tile(d),
        out_shape=jax.ShapeDtypeStruct((n, d), F32),
        scratch_shapes=[
            pltpu.VMEM((halo + tm, FF_CHUNK), F32),
            pltpu.VMEM((ff // FF_CHUNK, halo, FF_CHUNK), F32),
            pltpu.VMEM((tm, ff), BF16),
        ],
        compiler_params=pltpu.CompilerParams(
            dimension_semantics=("arbitrary",), vmem_limit_bytes=VMEM_LIMIT),
        name="outproj_ffn",
    )(x2, a, b, wo, gpost, gpre2, wg, wu, cw, cb, wd, gpost2)


def kernel(x, norm_mix_pre, norm_mix_post, norm_ffn_pre, norm_ffn_post, w_in, ln_v_gain, ln_v_bias,
           spatial_w, spatial_b, rel_bias, w_out, w_gate, w_up, conv_w, conv_b, w_down):
    batch, seq, d = x.shape
    depth = w_in.shape[0]
    ff = w_gate.shape[-1]
    assert seq % PROJ_TOKENS_PER_STEP == 0 and seq % FFN_TOKENS_PER_STEP == 0
    assert ff % FF_CHUNK == 0 and d == A_WIDTH + B_WIDTH
    assert seq // KEYS_BACK == max(DILATIONS)

    gmat = jnp.asarray(np.kron(np.eye(A_GROUPS), np.ones((HEAD_DIM, HEAD_DIM))), BF16)
    x2 = x.reshape(batch * seq, d)
    for l in range(depth):
        row = lambda p: p[l].reshape(1, -1).astype(F32)
        sb = jnp.repeat(spatial_b[l].astype(F32).T, HEAD_DIM, axis=1)
        a_out, q, k, v = _proj_call(
            x2, row(norm_mix_pre), w_in[l].astype(BF16), gmat,
            row(ln_v_gain), row(ln_v_bias), spatial_w[l].astype(F32), sb)
        b_out = _attn_call(rel_bias.astype(F32), q, k, v, batch, seq)
        x2 = _ffn_call(
            x2, a_out, b_out, w_out[l].astype(BF16), row(norm_mix_post), row(norm_ffn_pre),
            w_gate[l].astype(BF16), w_up[l].astype(BF16), conv_w[l].astype(F32), row(conv_b),
            w_down[l].astype(BF16), row(norm_ffn_post), seq)
    return x2.reshape(batch, seq, d)
```

```python
import functools
import math

import numpy as np
import jax
import jax.numpy as jnp
from jax import lax
from jax.experimental import pallas as pl
from jax.experimental.pallas import tpu as pltpu

F32 = jnp.float32
BF16 = jnp.bfloat16

HEAD_DIM = 64
A_GROUPS = 4
A_WIDTH = A_GROUPS * HEAD_DIM
B_HEADS = 12
B_WIDTH = B_HEADS * HEAD_DIM
CHUNK = 128
DILATIONS = (1, 4, 16)
KEYS_BACK = 128
NUM_BUCKETS = 32
MAX_DISTANCE = 2048
CONV_WIDTH = 3
NORM_EPS = 1e-6
NEG_INF = -1e30
LOG2E = 1.4426950408889634

LANES = 128
SUBLANES = 8
HEADS_PER_STEP = LANES // HEAD_DIM
FF_CHUNK = 256
PROJ_TOKENS_PER_STEP = 1024
ROWS_PER_PASS = 512
FFN_TOKENS_PER_STEP = 512
VMEM_LIMIT = 56 * 1024 * 1024


def _rms(x, g):
    ms = jnp.mean(x * x, axis=-1, keepdims=True)
    return x * lax.rsqrt(ms + NORM_EPS) * g


def _gelu_gate(x):
    k = math.sqrt(2.0 / math.pi)
    return 0.5 + 0.5 * jnp.tanh(x * (k + (k * 0.044715) * (x * x)))


def _dot(a, b):
    return jnp.dot(a, b, preferred_element_type=F32)


def _dot_nt(a, b):
    return lax.dot_general(a, b, (((1,), (1,)), ((), ())), preferred_element_type=F32)


def _proj_kernel(x_ref, g_ref, w_ref, gmat_ref, lng_ref, lnb_ref, sw_ref, sb_ref,
                 a_ref, q_ref, k_ref, v_ref, uv_ref):
    gmat = gmat_ref[...]
    row = lax.broadcasted_iota(jnp.int32, (CHUNK, CHUNK), 0)
    col = lax.broadcasted_iota(jnp.int32, (CHUNK, CHUNK), 1)
    tril = row >= col
    ws = [jnp.where(tril, sw_ref[g], 0.0).astype(BF16) for g in range(A_GROUPS)]
    grp = lax.broadcasted_iota(jnp.int32, (CHUNK, A_WIDTH), 1) // HEAD_DIM
    sb = sb_ref[...]
    o = 2 * A_WIDTH

    def group_sum(t):
        hi = t.astype(BF16)
        lo = (t - hi.astype(F32)).astype(BF16)
        return _dot(hi, gmat) + _dot(lo, gmat)

    n_pass = x_ref.shape[0] // ROWS_PER_PASS
    pass_rows = [pl.ds(p * ROWS_PER_PASS, ROWS_PER_PASS) for p in range(n_pass)]

    for rows in pass_rows:
        h = _rms(x_ref[rows, :], g_ref[...]).astype(BF16)
        q_ref[rows, :] = _dot(h, w_ref[:, o:o + B_WIDTH]) * (LOG2E / math.sqrt(HEAD_DIM))
        k_ref[rows, :] = _dot(h, w_ref[:, o + B_WIDTH:o + 2 * B_WIDTH])
        v_ref[rows, :] = _dot(h, w_ref[:, o + 2 * B_WIDTH:o + 3 * B_WIDTH])
        uv_ref[rows, :] = _dot(h, w_ref[:, 0:o])

    for rows in pass_rows:
        ur = uv_ref[rows, 0:A_WIDTH]
        u = ur * _gelu_gate(ur)
        vr = uv_ref[rows, A_WIDTH:o]
        vf = vr * _gelu_gate(vr)
        mu = group_sum(vf) * (1.0 / HEAD_DIM)
        dl = vf - mu
        var = group_sum(dl * dl) * (1.0 / HEAD_DIM)
        vn = (dl * lax.rsqrt(var + NORM_EPS) * lng_ref[...] + lnb_ref[...]).astype(BF16)

        for c in range(ROWS_PER_PASS // CHUNK):
            chunk = slice(c * CHUNK, (c + 1) * CHUNK)
            vc = vn[chunk]
            z = _dot(ws[0], vc)
            for g in range(1, A_GROUPS):
                z = jnp.where(grp == g, _dot(ws[g], vc), z)
            a_ref[pl.ds(rows.start + c * CHUNK, CHUNK), :] = (u[chunk] * (z + sb)).astype(BF16)


def _proj_call(x2, g_pre, w_in, gmat, lng, lnb, sw, sb):
    n, d = x2.shape
    tm = PROJ_TOKENS_PER_STEP
    cols = w_in.shape[1]
    const = lambda i: (0, 0)
    once = pl.Buffered(1)
    return pl.pallas_call(
        _proj_kernel,
        grid=(n // tm,),
        in_specs=[
            pl.BlockSpec((tm, d), lambda i: (i, 0)),
            pl.BlockSpec((1, d), const, pipeline_mode=once),
            pl.BlockSpec((d, cols), const, pipeline_mode=once),
            pl.BlockSpec((A_WIDTH, A_WIDTH), const, pipeline_mode=once),
            pl.BlockSpec((1, A_WIDTH), const, pipeline_mode=once),
            pl.BlockSpec((1, A_WIDTH), const, pipeline_mode=once),
            pl.BlockSpec((A_GROUPS, CHUNK, CHUNK), lambda i: (0, 0, 0), pipeline_mode=once),
            pl.BlockSpec((CHUNK, A_WIDTH), const, pipeline_mode=once),
        ],
        out_specs=[
            pl.BlockSpec((tm, A_WIDTH), lambda i: (i, 0)),
            pl.BlockSpec((tm, B_WIDTH), lambda i: (i, 0)),
            pl.BlockSpec((tm, B_WIDTH), lambda i: (i, 0)),
            pl.BlockSpec((tm, B_WIDTH), lambda i: (i, 0)),
        ],
        out_shape=[
            jax.ShapeDtypeStruct((n, A_WIDTH), BF16),
            jax.ShapeDtypeStruct((n, B_WIDTH), F32),
            jax.ShapeDtypeStruct((n, B_WIDTH), F32),
            jax.ShapeDtypeStruct((n, B_WIDTH), F32),
        ],
        scratch_shapes=[pltpu.VMEM((tm, 2 * A_WIDTH), F32)],
        compiler_params=pltpu.CompilerParams(
            dimension_semantics=("arbitrary",), vmem_limit_bytes=VMEM_LIMIT),
        name="proj_gmlp",
    )(x2, g_pre, w_in, gmat, lng, lnb, sw, sb)


def _t5_bucket_np(dist):
    max_exact = NUM_BUCKETS // 2
    d = np.maximum(dist, 1).astype(np.float64)
    large = max_exact + (np.log(d / max_exact) / math.log(MAX_DISTANCE / max_exact)
                         * (NUM_BUCKETS - max_exact))
    large = np.minimum(large.astype(np.int32), NUM_BUCKETS - 1)
    return np.where(dist < max_exact, dist, large)


def _bucket_starts(d):
    b = _t5_bucket_np(np.arange(KEYS_BACK + 1) * d)
    return [(r, int(b[r])) for r in range(KEYS_BACK + 1) if r == 0 or b[r] != b[r - 1]]


def _attn_kernel(rb_ref, q_ref, k_ref, v_ref, o_ref,
                 qs, kp, va, bt, stg, mb, lb, ab):
    seq = q_ref.shape[0]
    nblk = seq // KEYS_BACK
    padded = seq + KEYS_BACK
    assert DILATIONS == (1, 4, 16) and nblk == DILATIONS[-1]
    lane = lax.broadcasted_iota(jnp.int32, (KEYS_BACK, LANES), 1)
    first_head = lane < HEAD_DIM

    @pl.when(pl.program_id(1) == 0)
    def _():
        i = lax.broadcasted_iota(jnp.int32, (KEYS_BACK, 2 * KEYS_BACK), 0)
        j = lax.broadcasted_iota(jnp.int32, (KEYS_BACK, 2 * KEYS_BACK), 1)
        rel = KEYS_BACK + i - j
        band = (rel >= 0) & (rel <= KEYS_BACK)
        band_cur = band & (j >= KEYS_BACK)
        for di, d in enumerate(DILATIONS):
            for hh in range(HEADS_PER_STEP):
                head = pl.program_id(0) * HEADS_PER_STEP + hh
                val = jnp.zeros(rel.shape, F32)
                for start, bucket in _bucket_starts(d):
                    val = jnp.where(rel >= start, rb_ref[bucket, head], val)
                val = val * LOG2E
                rows = pl.ds(hh * KEYS_BACK, KEYS_BACK)
                bt[2 * di, rows, :] = jnp.where(band, val, NEG_INF)
                bt[2 * di + 1, rows, :] = jnp.where(band_cur, val, NEG_INF)

    d4 = DILATIONS[1]
    quarter = seq // d4

    def d4_rows(c):
        return pl.ds(c // d4 + (c % d4) * KEYS_BACK * d4, KEYS_BACK, stride=d4)

    def d16_rows_in_d4_order(r):
        return pl.ds((r % d4) * quarter + r // d4, KEYS_BACK, stride=d4)

    def put_block(j, key_row, qv, kv, vv):
        qs[j, 0:KEYS_BACK, :] = jnp.where(first_head, qv, 0.0).astype(BF16)
        qs[j, KEYS_BACK:, :] = jnp.where(first_head, 0.0, qv).astype(BF16)
        kp[pl.ds(key_row, KEYS_BACK), :] = kv.astype(BF16)
        va[pl.ds(key_row, KEYS_BACK), 0:LANES] = vv.astype(BF16)

    @pl.when((pl.program_id(0) == 0) & (pl.program_id(1) == 0))
    def _():
        for di in range(len(DILATIONS)):
            kp[pl.ds(di * padded, KEYS_BACK), :] = jnp.zeros((KEYS_BACK, LANES), BF16)
            va[pl.ds(di * padded, KEYS_BACK), :] = jnp.zeros((KEYS_BACK, 2 * LANES), BF16)
            va[pl.ds(di * padded + KEYS_BACK, seq), LANES:] = jnp.ones((seq, LANES), BF16)

    for g in range(nblk):
        rows = pl.ds(g * KEYS_BACK, KEYS_BACK)
        put_block(g, (g + 1) * KEYS_BACK, q_ref[rows, :], k_ref[rows, :], v_ref[rows, :])
    for g in range(nblk):
        src = d4_rows(g)
        rows = pl.ds(g * KEYS_BACK, KEYS_BACK)
        qv, kv, vv = q_ref[src, :], k_ref[src, :], v_ref[src, :]
        stg[0, rows, :] = qv
        stg[1, rows, :] = kv
        stg[2, rows, :] = vv
        put_block(nblk + g, padded + (g + 1) * KEYS_BACK, qv, kv, vv)
    for g in range(nblk):
        src = d16_rows_in_d4_order(g)
        put_block(2 * nblk + g, 2 * padded + (g + 1) * KEYS_BACK,
                  stg[0, src, :], stg[1, src, :], stg[2, src, :])

    def out_rows(j):
        if j // nblk == 2:
            r = j % nblk
            return pl.ds(2 * seq + (r % d4) * quarter + r // d4, KEYS_BACK, stride=d4)
        return pl.ds(j * KEYS_BACK, KEYS_BACK)

    def key_rows(j):
        return pl.ds((j // nblk) * padded + (j % nblk) * KEYS_BACK, 2 * KEYS_BACK)

    def qk(j):
        return _dot_nt(qs[j], kp[key_rows(j), :])

    def softmax(j, s):
        di, g = j // nblk, j % nblk
        no_prev = g % (nblk // DILATIONS[di]) == 0
        s = s + bt[2 * di + int(no_prev)]
        m = jnp.max(s, axis=-1, keepdims=True)
        mb[out_rows(j), :] = jnp.where(first_head, m[:KEYS_BACK], m[KEYS_BACK:])
        return jnp.exp2(s - m).astype(BF16)

    def pv(j, p):
        r = _dot(p, va[key_rows(j), :])
        ab[out_rows(j), :] = jnp.where(first_head, r[:KEYS_BACK, :LANES], r[KEYS_BACK:, :LANES])
        lb[out_rows(j), :] = jnp.where(first_head, r[:KEYS_BACK, LANES:], r[KEYS_BACK:, LANES:])

    n_blocks = len(DILATIONS) * nblk
    for j in range(n_blocks):
        pv(j, softmax(j, qk(j)))

    def combine(c, carry):
        tok = d4_rows(c)
        views = (tok,
                 pl.ds(pl.multiple_of(seq + c * KEYS_BACK, KEYS_BACK), KEYS_BACK),
                 pl.ds(pl.multiple_of(2 * seq + c * KEYS_BACK, KEYS_BACK), KEYS_BACK))
        ms = [mb[v, :] for v in views]
        mx = jnp.maximum(jnp.maximum(ms[0], ms[1]), ms[2])
        num = jnp.zeros((KEYS_BACK, LANES), F32)
        den = jnp.zeros((KEYS_BACK, LANES), F32)
        for m, v in zip(ms, views):
            e = jnp.exp2(m - mx)
            num = num + e * ab[v, :]
            den = den + e * lb[v, :]
        o_ref[tok, :] = num / den
        return carry

    lax.fori_loop(0, nblk, combine, 0)


def _attn_call(rel_bias, q, k, v, batch, seq):
    nd = len(DILATIONS)
    nblk = seq // KEYS_BACK
    n_pairs = B_HEADS // HEADS_PER_STEP
    io_spec = pl.BlockSpec((seq, LANES), lambda hp, b: (b, hp))
    return pl.pallas_call(
        _attn_kernel,
        grid=(n_pairs, batch),
        in_specs=[
            pl.BlockSpec(memory_space=pltpu.SMEM),
            io_spec, io_spec, io_spec,
        ],
        out_specs=io_spec,
        out_shape=jax.ShapeDtypeStruct((batch * seq, B_WIDTH), F32),
        scratch_shapes=[
            pltpu.VMEM((nd * nblk, 2 * KEYS_BACK, LANES), BF16),
            pltpu.VMEM((nd * (seq + KEYS_BACK), LANES), BF16),
            pltpu.VMEM((nd * (seq + KEYS_BACK), 2 * LANES), BF16),
            pltpu.VMEM((2 * nd, 2 * KEYS_BACK, 2 * KEYS_BACK), F32),
            pltpu.VMEM((3, seq, LANES), F32),
            pltpu.VMEM((nd * seq, LANES), F32),
            pltpu.VMEM((nd * seq, LANES), F32),
            pltpu.VMEM((nd * seq, LANES), F32),
        ],
        compiler_params=pltpu.CompilerParams(
            dimension_semantics=("arbitrary", "arbitrary"), vmem_limit_bytes=VMEM_LIMIT),
        name="dilated_attn",
    )(rel_bias, q, k, v)


def _ffn_kernel(x_ref, a_ref, b_ref, wo_ref, gpost_ref, gpre2_ref, wg_ref, wu_ref, cw_ref, cb_ref,
                wd_ref, gpost2_ref, o_ref, carry, act_buf, *, tiles_per_seq):
    tm = x_ref.shape[0]
    halo = carry.shape[1]
    n_chunks = wd_ref.shape[0] // FF_CHUNK
    first = pl.program_id(0) % tiles_per_seq == 0

    @pl.when(first)
    def _():
        carry[...] = jnp.zeros(carry.shape, F32)

    y1 = (_dot(a_ref[...], wo_ref[0:A_WIDTH, :])
          + _dot(b_ref[...].astype(BF16), wo_ref[A_WIDTH:, :]))
    x1 = x_ref[...] + _rms(y1, gpost_ref[...])
    h2 = _rms(x1, gpre2_ref[...]).astype(BF16)

    def gate_up(c):
        cols = slice(c * FF_CHUNK, (c + 1) * FF_CHUNK)
        return _dot(h2, wg_ref[:, cols]), _dot(h2, wu_ref[:, cols])

    gu = gate_up(0)
    for c in range(n_chunks):
        cols = slice(c * FF_CHUNK, (c + 1) * FF_CHUNK)
        gu_next = gate_up(c + 1) if c + 1 < n_chunks else None
        g, up = gu
        ext = jnp.concatenate([carry[c], g], axis=0)
        carry[c] = g[tm - halo:, :]
        cw = cw_ref[:, cols]
        cv = (cb_ref[:, cols]
              + pltpu.roll(ext, 2, axis=0)[halo:] * cw[0:1]
              + pltpu.roll(ext, 1, axis=0)[halo:] * cw[1:2]
              + g * cw[2:3])
        act_buf[:, cols] = ((cv * up) * _gelu_gate(cv)).astype(BF16)
        gu = gu_next
    o_ref[...] = x1 + _rms(_dot(act_buf[...], wd_ref[...]), gpost2_ref[...])


def _ffn_call(x2, a, b, wo, gpost, gpre2, wg, wu, cw, cb, wd, gpost2, seq):
    n, d = x2.shape
    tm = FFN_TOKENS_PER_STEP
    ff = wd.shape[0]
    halo = SUBLANES
    const = lambda i: (0, 0)
    tile = lambda w: pl.BlockSpec((tm, w), lambda i: (i, 0))
    return pl.pallas_call(
        functools.partial(_ffn_kernel, tiles_per_seq=seq // tm),
        grid=(n // tm,),
        in_specs=[
            tile(d), tile(A_WIDTH), tile(B_WIDTH),
            pl.BlockSpec((d, d), const),
            pl.BlockSpec((1, d), const),
            pl.BlockSpec((1, d), const),
            pl.BlockSpec((d, ff), const),
            pl.BlockSpec((d, ff), const),
            pl.BlockSpec((CONV_WIDTH, ff), const),
            pl.BlockSpec((1, ff), const),
            pl.BlockSpec((ff, d), const),
            pl.BlockSpec((1, d), const),
        ],
        out_specs=tile(d),
        out_shape=jax.ShapeDtypeStruct((n, d), F32),
        scratch_shapes=[
            pltpu.VMEM((ff // FF_CHUNK, halo, FF_CHUNK), F32),
            pltpu.VMEM((tm, ff), BF16),
        ],
        compiler_params=pltpu.CompilerParams(
            dimension_semantics=("arbitrary",), vmem_limit_bytes=VMEM_LIMIT),
        name="outproj_ffn",
    )(x2, a, b, wo, gpost, gpre2, wg, wu, cw, cb, wd, gpost2)


def kernel(x, norm_mix_pre, norm_mix_post, norm_ffn_pre, norm_ffn_post, w_in, ln_v_gain, ln_v_bias,
           spatial_w, spatial_b, rel_bias, w_out, w_gate, w_up, conv_w, conv_b, w_down):
    batch, seq, d = x.shape
    depth = w_in.shape[0]
    ff = w_gate.shape[-1]
    assert seq % PROJ_TOKENS_PER_STEP == 0 and seq % FFN_TOKENS_PER_STEP == 0
    assert ff % FF_CHUNK == 0 and d == A_WIDTH + B_WIDTH
    assert seq // KEYS_BACK == max(DILATIONS)

    gmat = jnp.asarray(np.kron(np.eye(A_GROUPS), np.ones((HEAD_DIM, HEAD_DIM))), BF16)
    x2 = x.reshape(batch * seq, d)
    for l in range(depth):
        row = lambda p: p[l].reshape(1, -1).astype(F32)
        sb = jnp.repeat(spatial_b[l].astype(F32).T, HEAD_DIM, axis=1)
        a_out, q, k, v = _proj_call(
            x2, row(norm_mix_pre), w_in[l].astype(BF16), gmat,
            row(ln_v_gain), row(ln_v_bias), spatial_w[l].astype(F32), sb)
        b_out = _attn_call(rel_bias.astype(F32), q, k, v, batch, seq)
        x2 = _ffn_call(
            x2, a_out, b_out, w_out[l].astype(BF16), row(norm_mix_post), row(norm_ffn_pre),
            w_gate[l].astype(BF16), w_up[l].astype(BF16), conv_w[l].astype(F32), row(conv_b),
            w_down[l].astype(BF16), row(norm_ffn_post), seq)
    return x2.reshape(batch, seq, d)
```

```python
import functools
import math

import numpy as np
import jax
import jax.numpy as jnp
from jax import lax
from jax.experimental import pallas as pl
from jax.experimental.pallas import tpu as pltpu

F32 = jnp.float32
BF16 = jnp.bfloat16

HEAD_DIM = 64
A_GROUPS = 4
A_WIDTH = A_GROUPS * HEAD_DIM
B_HEADS = 12
B_WIDTH = B_HEADS * HEAD_DIM
CHUNK = 128
DILATIONS = (1, 4, 16)
KEYS_BACK = 128
NUM_BUCKETS = 32
MAX_DISTANCE = 2048
CONV_WIDTH = 3
NORM_EPS = 1e-6
NEG_INF = -1e30
LOG2E = 1.4426950408889634

LANES = 128
SUBLANES = 8
HEADS_PER_STEP = LANES // HEAD_DIM
FF_CHUNK = 256
PROJ_TOKENS_PER_STEP = 1024
ROWS_PER_PASS = 512
FFN_TOKENS_PER_STEP = 512
WEIGHT_ROWS_PER_COPY = 256
VMEM_LIMIT = 56 * 1024 * 1024


def _rms(x, g):
    ms = jnp.mean(x * x, axis=-1, keepdims=True)
    return x * lax.rsqrt(ms + NORM_EPS) * g


def _dot(a, b):
    return jnp.dot(a, b, preferred_element_type=F32)


def _dot_nt(a, b):
    return lax.dot_general(a, b, (((1,), (1,)), ((), ())), preferred_element_type=F32)


def _load_weight_as_bf16(w_hbm, dst, stage, sems):
    rows = stage.shape[1]
    n_chunks = w_hbm.shape[0] // rows

    def chunk_copy(c):
        return pltpu.make_async_copy(
            w_hbm.at[pl.ds(c * rows, rows), :], stage.at[c % 2], sems.at[c % 2])

    chunk_copy(0).start()
    for c in range(n_chunks):
        if c + 1 < n_chunks:
            chunk_copy(c + 1).start()
        chunk_copy(c).wait()
        dst[pl.ds(c * rows, rows), :] = stage[c % 2].astype(BF16)


def _proj_kernel(x_ref, g_ref, w_hbm, gmat_ref, lng_ref, lnb_ref, sw_ref, sb_ref,
                 a_ref, q_ref, k_ref, v_ref, uv_ref, w_ref, w_stage, w_sems):
    @pl.when(pl.program_id(0) == 0)
    def _():
        _load_weight_as_bf16(w_hbm, w_ref, w_stage, w_sems)

    gmat = gmat_ref[...]
    row = lax.broadcasted_iota(jnp.int32, (CHUNK, CHUNK), 0)
    col = lax.broadcasted_iota(jnp.int32, (CHUNK, CHUNK), 1)
    tril = row >= col
    ws = [jnp.where(tril, sw_ref[g], 0.0).astype(BF16) for g in range(A_GROUPS)]
    grp = lax.broadcasted_iota(jnp.int32, (CHUNK, A_WIDTH), 1) // HEAD_DIM
    sb = sb_ref[...]
    o = 2 * A_WIDTH

    def group_sum(t):
        hi = t.astype(BF16)
        lo = (t - hi.astype(F32)).astype(BF16)
        return _dot(hi, gmat) + _dot(lo, gmat)

    n_pass = x_ref.shape[0] // ROWS_PER_PASS
    pass_rows = [pl.ds(p * ROWS_PER_PASS, ROWS_PER_PASS) for p in range(n_pass)]

    for rows in pass_rows:
        h = _rms(x_ref[rows, :], g_ref[...]).astype(BF16)
        q_ref[rows, :] = _dot(h, w_ref[:, o:o + B_WIDTH]) * (LOG2E / math.sqrt(HEAD_DIM))
        k_ref[rows, :] = _dot(h, w_ref[:, o + B_WIDTH:o + 2 * B_WIDTH])
        v_ref[rows, :] = _dot(h, w_ref[:, o + 2 * B_WIDTH:o + 3 * B_WIDTH])
        uv_ref[rows, :] = _dot(h, w_ref[:, 0:o])

    for rows in pass_rows:
        u = jax.nn.gelu(uv_ref[rows, 0:A_WIDTH])
        vf = jax.nn.gelu(uv_ref[rows, A_WIDTH:o])
        mu = group_sum(vf) * (1.0 / HEAD_DIM)
        dl = vf - mu
        var = group_sum(dl * dl) * (1.0 / HEAD_DIM)
        vn = (dl * lax.rsqrt(var + NORM_EPS) * lng_ref[...] + lnb_ref[...]).astype(BF16)

        for c in range(ROWS_PER_PASS // CHUNK):
            chunk = slice(c * CHUNK, (c + 1) * CHUNK)
            vc = vn[chunk]
            z = _dot(ws[0], vc)
            for g in range(1, A_GROUPS):
                z = jnp.where(grp == g, _dot(ws[g], vc), z)
            a_ref[pl.ds(rows.start + c * CHUNK, CHUNK), :] = (u[chunk] * (z + sb)).astype(BF16)


def _proj_call(x2, g_pre, w_in, gmat, lng, lnb, sw, sb):
    n, d = x2.shape
    tm = PROJ_TOKENS_PER_STEP
    cols = w_in.shape[1]
    const = lambda i: (0, 0)
    once = pl.Buffered(1)
    return pl.pallas_call(
        _proj_kernel,
        grid=(n // tm,),
        in_specs=[
            pl.BlockSpec((tm, d), lambda i: (i, 0)),
            pl.BlockSpec((1, d), const, pipeline_mode=once),
            pl.BlockSpec(memory_space=pl.ANY),
            pl.BlockSpec((A_WIDTH, A_WIDTH), const, pipeline_mode=once),
            pl.BlockSpec((1, A_WIDTH), const, pipeline_mode=once),
            pl.BlockSpec((1, A_WIDTH), const, pipeline_mode=once),
            pl.BlockSpec((A_GROUPS, CHUNK, CHUNK), lambda i: (0, 0, 0), pipeline_mode=once),
            pl.BlockSpec((CHUNK, A_WIDTH), const, pipeline_mode=once),
        ],
        out_specs=[
            pl.BlockSpec((tm, A_WIDTH), lambda i: (i, 0)),
            pl.BlockSpec((tm, B_WIDTH), lambda i: (i, 0)),
            pl.BlockSpec((tm, B_WIDTH), lambda i: (i, 0)),
            pl.BlockSpec((tm, B_WIDTH), lambda i: (i, 0)),
        ],
        out_shape=[
            jax.ShapeDtypeStruct((n, A_WIDTH), BF16),
            jax.ShapeDtypeStruct((n, B_WIDTH), F32),
            jax.ShapeDtypeStruct((n, B_WIDTH), F32),
            jax.ShapeDtypeStruct((n, B_WIDTH), F32),
        ],
        scratch_shapes=[
            pltpu.VMEM((tm, 2 * A_WIDTH), F32),
            pltpu.VMEM((d, cols), BF16),
            pltpu.VMEM((2, WEIGHT_ROWS_PER_COPY, cols), F32),
            pltpu.SemaphoreType.DMA((2,)),
        ],
        compiler_params=pltpu.CompilerParams(
            dimension_semantics=("arbitrary",), vmem_limit_bytes=VMEM_LIMIT),
        name="proj_gmlp",
    )(x2, g_pre, w_in, gmat, lng, lnb, sw, sb)


def _t5_bucket_np(dist):
    max_exact = NUM_BUCKETS // 2
    d = np.maximum(dist, 1).astype(np.float64)
    large = max_exact + (np.log(d / max_exact) / math.log(MAX_DISTANCE / max_exact)
                         * (NUM_BUCKETS - max_exact))
    large = np.minimum(large.astype(np.int32), NUM_BUCKETS - 1)
    return np.where(dist < max_exact, dist, large)


def _bucket_starts(d):
    b = _t5_bucket_np(np.arange(KEYS_BACK + 1) * d)
    return [(r, int(b[r])) for r in range(KEYS_BACK + 1) if r == 0 or b[r] != b[r - 1]]


def _attn_kernel(rb_ref, q_ref, k_ref, v_ref, o_ref,
                 qs, kp, va, bt, stg, mb, lb, ab):
    seq = q_ref.shape[0]
    nblk = seq // KEYS_BACK
    padded = seq + KEYS_BACK
    assert DILATIONS == (1, 4, 16) and nblk == DILATIONS[-1]
    lane = lax.broadcasted_iota(jnp.int32, (KEYS_BACK, LANES), 1)
    first_head = lane < HEAD_DIM

    @pl.when(pl.program_id(1) == 0)
    def _():
        i = lax.broadcasted_iota(jnp.int32, (KEYS_BACK, 2 * KEYS_BACK), 0)
        j = lax.broadcasted_iota(jnp.int32, (KEYS_BACK, 2 * KEYS_BACK), 1)
        rel = KEYS_BACK + i - j
        band = (rel >= 0) & (rel <= KEYS_BACK)
        band_cur = band & (j >= KEYS_BACK)
        for di, d in enumerate(DILATIONS):
            for hh in range(HEADS_PER_STEP):
                head = pl.program_id(0) * HEADS_PER_STEP + hh
                val = jnp.zeros(rel.shape, F32)
                for start, bucket in _bucket_starts(d):
                    val = jnp.where(rel >= start, rb_ref[bucket, head], val)
                val = val * LOG2E
                rows = pl.ds(hh * KEYS_BACK, KEYS_BACK)
                bt[2 * di, rows, :] = jnp.where(band, val, NEG_INF)
                bt[2 * di + 1, rows, :] = jnp.where(band_cur, val, NEG_INF)

    d4 = DILATIONS[1]
    quarter = seq // d4

    def d4_rows(c):
        return pl.ds(c // d4 + (c % d4) * KEYS_BACK * d4, KEYS_BACK, stride=d4)

    def d16_rows_in_d4_order(r):
        return pl.ds((r % d4) * quarter + r // d4, KEYS_BACK, stride=d4)

    def put_block(j, key_row, qv, kv, vv):
        qs[j, 0:KEYS_BACK, :] = jnp.where(first_head, qv, 0.0).astype(BF16)
        qs[j, KEYS_BACK:, :] = jnp.where(first_head, 0.0, qv).astype(BF16)
        kp[pl.ds(key_row, KEYS_BACK), :] = kv.astype(BF16)
        va[pl.ds(key_row, KEYS_BACK), 0:LANES] = vv.astype(BF16)

    @pl.when((pl.program_id(0) == 0) & (pl.program_id(1) == 0))
    def _():
        for di in range(len(DILATIONS)):
            kp[pl.ds(di * padded, KEYS_BACK), :] = jnp.zeros((KEYS_BACK, LANES), BF16)
            va[pl.ds(di * padded, KEYS_BACK), :] = jnp.zeros((KEYS_BACK, 2 * LANES), BF16)
            va[pl.ds(di * padded + KEYS_BACK, seq), LANES:] = jnp.ones((seq, LANES), BF16)

    for g in range(nblk):
        rows = pl.ds(g * KEYS_BACK, KEYS_BACK)
        put_block(g, (g + 1) * KEYS_BACK, q_ref[rows, :], k_ref[rows, :], v_ref[rows, :])
    for g in range(nblk):
        src = d4_rows(g)
        rows = pl.ds(g * KEYS_BACK, KEYS_BACK)
        qv, kv, vv = q_ref[src, :], k_ref[src, :], v_ref[src, :]
        stg[0, rows, :] = qv
        stg[1, rows, :] = kv
        stg[2, rows, :] = vv
        put_block(nblk + g, padded + (g + 1) * KEYS_BACK, qv, kv, vv)
    for g in range(nblk):
        src = d16_rows_in_d4_order(g)
        put_block(2 * nblk + g, 2 * padded + (g + 1) * KEYS_BACK,
                  stg[0, src, :], stg[1, src, :], stg[2, src, :])

    def out_rows(j):
        if j // nblk == 2:
            r = j % nblk
            return pl.ds(2 * seq + (r % d4) * quarter + r // d4, KEYS_BACK, stride=d4)
        return pl.ds(j * KEYS_BACK, KEYS_BACK)

    def key_rows(j):
        return pl.ds((j // nblk) * padded + (j % nblk) * KEYS_BACK, 2 * KEYS_BACK)

    def qk(j):
        return _dot_nt(qs[j], kp[key_rows(j), :])

    def softmax(j, s):
        di, g = j // nblk, j % nblk
        no_prev = g % (nblk // DILATIONS[di]) == 0
        s = s + bt[2 * di + int(no_prev)]
        m = jnp.max(s, axis=-1, keepdims=True)
        mb[out_rows(j), :] = jnp.where(first_head, m[:KEYS_BACK], m[KEYS_BACK:])
        return jnp.exp2(s - m).astype(BF16)

    def pv(j, p):
        r = _dot(p, va[key_rows(j), :])
        ab[out_rows(j), :] = jnp.where(first_head, r[:KEYS_BACK, :LANES], r[KEYS_BACK:, :LANES])
        lb[out_rows(j), :] = jnp.where(first_head, r[:KEYS_BACK, LANES:], r[KEYS_BACK:, LANES:])

    n_blocks = len(DILATIONS) * nblk
    for j in range(n_blocks):
        pv(j, softmax(j, qk(j)))

    def combine(c, carry):
        tok = d4_rows(c)
        views = (tok,
                 pl.ds(pl.multiple_of(seq + c * KEYS_BACK, KEYS_BACK), KEYS_BACK),
                 pl.ds(pl.multiple_of(2 * seq + c * KEYS_BACK, KEYS_BACK), KEYS_BACK))
        ms = [mb[v, :] for v in views]
        mx = jnp.maximum(jnp.maximum(ms[0], ms[1]), ms[2])
        num = jnp.zeros((KEYS_BACK, LANES), F32)
        den = jnp.zeros((KEYS_BACK, LANES), F32)
        for m, v in zip(ms, views):
            e = jnp.exp2(m - mx)
            num = num + e * ab[v, :]
            den = den + e * lb[v, :]
        o_ref[tok, :] = num / den
        return carry

    lax.fori_loop(0, nblk, combine, 0)


def _attn_call(rel_bias, q, k, v, batch, seq):
    nd = len(DILATIONS)
    nblk = seq // KEYS_BACK
    n_pairs = B_HEADS // HEADS_PER_STEP
    io_spec = pl.BlockSpec((seq, LANES), lambda hp, b: (b, hp))
    return pl.pallas_call(
        _attn_kernel,
        grid=(n_pairs, batch),
        in_specs=[
            pl.BlockSpec(memory_space=pltpu.SMEM),
            io_spec, io_spec, io_spec,
        ],
        out_specs=io_spec,
        out_shape=jax.ShapeDtypeStruct((batch * seq, B_WIDTH), F32),
        scratch_shapes=[
            pltpu.VMEM((nd * nblk, 2 * KEYS_BACK, LANES), BF16),
            pltpu.VMEM((nd * (seq + KEYS_BACK), LANES), BF16),
            pltpu.VMEM((nd * (seq + KEYS_BACK), 2 * LANES), BF16),
            pltpu.VMEM((2 * nd, 2 * KEYS_BACK, 2 * KEYS_BACK), F32),
            pltpu.VMEM((3, seq, LANES), F32),
            pltpu.VMEM((nd * seq, LANES), F32),
            pltpu.VMEM((nd * seq, LANES), F32),
            pltpu.VMEM((nd * seq, LANES), F32),
        ],
        compiler_params=pltpu.CompilerParams(
            dimension_semantics=("arbitrary", "arbitrary"), vmem_limit_bytes=VMEM_LIMIT),
        name="dilated_attn",
    )(rel_bias, q, k, v)


def _ffn_kernel(x_ref, a_ref, b_ref, wo_hbm, gpost_ref, gpre2_ref, wg_hbm, wu_hbm, cw_ref, cb_ref,
                wd_hbm, gpost2_ref, o_ref, gbuf, carry, act_buf,
                wo_ref, wg_ref, wu_ref, wd_ref, wide_stage, narrow_stage, sems, *, tiles_per_seq):
    @pl.when(pl.program_id(0) == 0)
    def _():
        _load_weight_as_bf16(wo_hbm, wo_ref, narrow_stage, sems)
        _load_weight_as_bf16(wg_hbm, wg_ref, wide_stage, sems)
        _load_weight_as_bf16(wu_hbm, wu_ref, wide_stage, sems)
        _load_weight_as_bf16(wd_hbm, wd_ref, narrow_stage, sems)

    tm = x_ref.shape[0]
    halo = carry.shape[1]
    n_chunks = wd_ref.shape[0] // FF_CHUNK
    first = pl.program_id(0) % tiles_per_seq == 0

    @pl.when(first)
    def _():
        carry[...] = jnp.zeros(carry.shape, F32)

    y1 = (_dot(a_ref[...], wo_ref[0:A_WIDTH, :])
          + _dot(b_ref[...].astype(BF16), wo_ref[A_WIDTH:, :]))
    x1 = x_ref[...] + _rms(y1, gpost_ref[...])
    h2 = _rms(x1, gpre2_ref[...]).astype(BF16)

    def gate_up(c):
        cols = slice(c * FF_CHUNK, (c + 1) * FF_CHUNK)
        return _dot(h2, wg_ref[:, cols]), _dot(h2, wu_ref[:, cols])

    gu = gate_up(0)
    for c in range(n_chunks):
        cols = slice(c * FF_CHUNK, (c + 1) * FF_CHUNK)
        gu_next = gate_up(c + 1) if c + 1 < n_chunks else None
        g, up = gu
        gbuf[0:halo, :] = carry[c]
        gbuf[halo:halo + tm, :] = g
        carry[c] = g[tm - halo:, :]
        cw = cw_ref[:, cols]
        cv = (cb_ref[:, cols]
              + gbuf[halo - 2:halo - 2 + tm, :] * cw[0:1]
              + gbuf[halo - 1:halo - 1 + tm, :] * cw[1:2]
              + g * cw[2:3])
        act_buf[:, cols] = (jax.nn.gelu(cv) * up).astype(BF16)
        gu = gu_next
    o_ref[...] = x1 + _rms(_dot(act_buf[...], wd_ref[...]), gpost2_ref[...])


def _ffn_call(x2, a, b, wo, gpost, gpre2, wg, wu, cw, cb, wd, gpost2, seq):
    n, d = x2.shape
    tm = FFN_TOKENS_PER_STEP
    ff = wd.shape[0]
    halo = SUBLANES
    const = lambda i: (0, 0)
    tile = lambda w: pl.BlockSpec((tm, w), lambda i: (i, 0))
    in_hbm = pl.BlockSpec(memory_space=pl.ANY)
    return pl.pallas_call(
        functools.partial(_ffn_kernel, tiles_per_seq=seq // tm),
        grid=(n // tm,),
        in_specs=[
            tile(d), tile(A_WIDTH), tile(B_WIDTH),
            in_hbm,
            pl.BlockSpec((1, d), const),
            pl.BlockSpec((1, d), const),
            in_hbm, in_hbm,
            pl.BlockSpec((CONV_WIDTH, ff), const),
            pl.BlockSpec((1, ff), const),
            in_hbm,
            pl.BlockSpec((1, d), const),
        ],
        out_specs=tile(d),
        out_shape=jax.ShapeDtypeStruct((n, d), F32),
        scratch_shapes=[
            pltpu.VMEM((halo + tm, FF_CHUNK), F32),
            pltpu.VMEM((ff // FF_CHUNK, halo, FF_CHUNK), F32),
            pltpu.VMEM((tm, ff), BF16),
            pltpu.VMEM((d, d), BF16),
            pltpu.VMEM((d, ff), BF16),
            pltpu.VMEM((d, ff), BF16),
            pltpu.VMEM((ff, d), BF16),
            pltpu.VMEM((2, WEIGHT_ROWS_PER_COPY, ff), F32),
            pltpu.VMEM((2, WEIGHT_ROWS_PER_COPY, d), F32),
            pltpu.SemaphoreType.DMA((2,)),
        ],
        compiler_params=pltpu.CompilerParams(
            dimension_semantics=("arbitrary",), vmem_limit_bytes=VMEM_LIMIT),
        name="outproj_ffn",
    )(x2, a, b, wo, gpost, gpre2, wg, wu, cw, cb, wd, gpost2)


def kernel(x, norm_mix_pre, norm_mix_post, norm_ffn_pre, norm_ffn_post, w_in, ln_v_gain, ln_v_bias,
           spatial_w, spatial_b, rel_bias, w_out, w_gate, w_up, conv_w, conv_b, w_down):
    batch, seq, d = x.shape
    depth = w_in.shape[0]
    ff = w_gate.shape[-1]
    assert seq % PROJ_TOKENS_PER_STEP == 0 and seq % FFN_TOKENS_PER_STEP == 0
    assert ff % FF_CHUNK == 0 and d == A_WIDTH + B_WIDTH
    assert seq // KEYS_BACK == max(DILATIONS)

    gmat = jnp.asarray(np.kron(np.eye(A_GROUPS), np.ones((HEAD_DIM, HEAD_DIM))), BF16)
    x2 = x.reshape(batch * seq, d)
    for l in range(depth):
        row = lambda p: p[l].reshape(1, -1).astype(F32)
        sb = jnp.repeat(spatial_b[l].astype(F32).T, HEAD_DIM, axis=1)
        a_out, q, k, v = _proj_call(
            x2, row(norm_mix_pre), w_in[l].astype(F32), gmat,
            row(ln_v_gain), row(ln_v_bias), spatial_w[l].astype(F32), sb)
        b_out = _attn_call(rel_bias.astype(F32), q, k, v, batch, seq)
        x2 = _ffn_call(
            x2, a_out, b_out, w_out[l].astype(F32), row(norm_mix_post), row(norm_ffn_pre),
            w_gate[l].astype(F32), w_up[l].astype(F32), conv_w[l].astype(F32), row(conv_b),
            w_down[l].astype(F32), row(norm_ffn_post), seq)
    return x2.reshape(batch, seq, d)
```

```python
import functools
import math

import numpy as np
import jax
import jax.numpy as jnp
from jax import lax
from jax.experimental import pallas as pl
from jax.experimental.pallas import tpu as pltpu

F32 = jnp.float32
BF16 = jnp.bfloat16

HEAD_DIM = 64
A_GROUPS = 4
A_WIDTH = A_GROUPS * HEAD_DIM
B_HEADS = 12
B_WIDTH = B_HEADS * HEAD_DIM
CHUNK = 128
DILATIONS = (1, 4, 16)
KEYS_BACK = 128
NUM_BUCKETS = 32
MAX_DISTANCE = 2048
CONV_WIDTH = 3
NORM_EPS = 1e-6
NEG_INF = -1e30
LOG2E = 1.4426950408889634

LANES = 128
SUBLANES = 8
HEADS_PER_STEP = LANES // HEAD_DIM
FF_CHUNK = 256
PROJ_TOKENS_PER_STEP = 1024
ROWS_PER_PASS = 512
FFN_TOKENS_PER_STEP = 512
WEIGHT_ROWS_PER_COPY = 256
N_LATER_WEIGHTS = 3
VMEM_LIMIT = 56 * 1024 * 1024


def _rms(x, g):
    ms = jnp.mean(x * x, axis=-1, keepdims=True)
    return x * lax.rsqrt(ms + NORM_EPS) * g


def _dot(a, b):
    return jnp.dot(a, b, preferred_element_type=F32)


def _dot_nt(a, b):
    return lax.dot_general(a, b, (((1,), (1,)), ((), ())), preferred_element_type=F32)


def _load_weight_as_bf16(w_hbm, dst, stage, sems):
    rows = stage.shape[1]
    n_chunks = w_hbm.shape[0] // rows

    def chunk_copy(c):
        return pltpu.make_async_copy(
            w_hbm.at[pl.ds(c * rows, rows), :], stage.at[c % 2], sems.at[c % 2])

    chunk_copy(0).start()
    for c in range(n_chunks):
        if c + 1 < n_chunks:
            chunk_copy(c + 1).start()
        chunk_copy(c).wait()
        dst[pl.ds(c * rows, rows), :] = stage[c % 2].astype(BF16)


def _proj_kernel(x_ref, g_ref, w_hbm, gmat_ref, lng_ref, lnb_ref, sw_ref, sb_ref, *rest):
    later_f32, rest = rest[:N_LATER_WEIGHTS], rest[N_LATER_WEIGHTS:]
    a_ref, q_ref, k_ref, v_ref = rest[:4]
    later_bf16, (uv_ref, w_ref, w_stage, w_sems) = rest[4:4 + N_LATER_WEIGHTS], rest[4 + N_LATER_WEIGHTS:]

    @pl.when(pl.program_id(0) == 0)
    def _():
        _load_weight_as_bf16(w_hbm, w_ref, w_stage, w_sems)

    for src, dst in zip(later_f32, later_bf16):
        dst[...] = src[...].astype(BF16)

    gmat = gmat_ref[...]
    row = lax.broadcasted_iota(jnp.int32, (CHUNK, CHUNK), 0)
    col = lax.broadcasted_iota(jnp.int32, (CHUNK, CHUNK), 1)
    tril = row >= col
    ws = [jnp.where(tril, sw_ref[g], 0.0).astype(BF16) for g in range(A_GROUPS)]
    grp = lax.broadcasted_iota(jnp.int32, (CHUNK, A_WIDTH), 1) // HEAD_DIM
    sb = sb_ref[...]
    o = 2 * A_WIDTH

    def group_sum(t):
        hi = t.astype(BF16)
        lo = (t - hi.astype(F32)).astype(BF16)
        return _dot(hi, gmat) + _dot(lo, gmat)

    n_pass = x_ref.shape[0] // ROWS_PER_PASS
    pass_rows = [pl.ds(p * ROWS_PER_PASS, ROWS_PER_PASS) for p in range(n_pass)]

    for rows in pass_rows:
        h = _rms(x_ref[rows, :], g_ref[...]).astype(BF16)
        q_ref[rows, :] = _dot(h, w_ref[:, o:o + B_WIDTH]) * (LOG2E / math.sqrt(HEAD_DIM))
        k_ref[rows, :] = _dot(h, w_ref[:, o + B_WIDTH:o + 2 * B_WIDTH])
        v_ref[rows, :] = _dot(h, w_ref[:, o + 2 * B_WIDTH:o + 3 * B_WIDTH])
        uv_ref[rows, :] = _dot(h, w_ref[:, 0:o])

    for rows in pass_rows:
        u = jax.nn.gelu(uv_ref[rows, 0:A_WIDTH])
        vf = jax.nn.gelu(uv_ref[rows, A_WIDTH:o])
        mu = group_sum(vf) * (1.0 / HEAD_DIM)
        dl = vf - mu
        var = group_sum(dl * dl) * (1.0 / HEAD_DIM)
        vn = (dl * lax.rsqrt(var + NORM_EPS) * lng_ref[...] + lnb_ref[...]).astype(BF16)

        for c in range(ROWS_PER_PASS // CHUNK):
            chunk = slice(c * CHUNK, (c + 1) * CHUNK)
            vc = vn[chunk]
            z = _dot(ws[0], vc)
            for g in range(1, A_GROUPS):
                z = jnp.where(grp == g, _dot(ws[g], vc), z)
            a_ref[pl.ds(rows.start + c * CHUNK, CHUNK), :] = (u[chunk] * (z + sb)).astype(BF16)


def _proj_call(x2, g_pre, w_in, gmat, lng, lnb, sw, sb, later_weights):
    n, d = x2.shape
    tm = PROJ_TOKENS_PER_STEP
    cols = w_in.shape[1]
    steps = n // tm
    const = lambda i: (0, 0)
    once = pl.Buffered(1)
    assert len(later_weights) == N_LATER_WEIGHTS
    slices = [pl.BlockSpec((w.shape[0] // steps, w.shape[1]), lambda i: (i, 0)) for w in later_weights]
    assert all(w.shape[0] % (steps * 2 * SUBLANES) == 0 for w in later_weights)
    return pl.pallas_call(
        _proj_kernel,
        grid=(n // tm,),
        in_specs=[
            pl.BlockSpec((tm, d), lambda i: (i, 0)),
            pl.BlockSpec((1, d), const, pipeline_mode=once),
            pl.BlockSpec(memory_space=pl.ANY),
            pl.BlockSpec((A_WIDTH, A_WIDTH), const, pipeline_mode=once),
            pl.BlockSpec((1, A_WIDTH), const, pipeline_mode=once),
            pl.BlockSpec((1, A_WIDTH), const, pipeline_mode=once),
            pl.BlockSpec((A_GROUPS, CHUNK, CHUNK), lambda i: (0, 0, 0), pipeline_mode=once),
            pl.BlockSpec((CHUNK, A_WIDTH), const, pipeline_mode=once),
        ] + slices,
        out_specs=[
            pl.BlockSpec((tm, A_WIDTH), lambda i: (i, 0)),
            pl.BlockSpec((tm, B_WIDTH), lambda i: (i, 0)),
            pl.BlockSpec((tm, B_WIDTH), lambda i: (i, 0)),
            pl.BlockSpec((tm, B_WIDTH), lambda i: (i, 0)),
        ] + slices,
        out_shape=[
            jax.ShapeDtypeStruct((n, A_WIDTH), BF16),
            jax.ShapeDtypeStruct((n, B_WIDTH), F32),
            jax.ShapeDtypeStruct((n, B_WIDTH), F32),
            jax.ShapeDtypeStruct((n, B_WIDTH), F32),
        ] + [jax.ShapeDtypeStruct(w.shape, BF16) for w in later_weights],
        scratch_shapes=[
            pltpu.VMEM((tm, 2 * A_WIDTH), F32),
            pltpu.VMEM((d, cols), BF16),
            pltpu.VMEM((2, WEIGHT_ROWS_PER_COPY, cols), F32),
            pltpu.SemaphoreType.DMA((2,)),
        ],
        compiler_params=pltpu.CompilerParams(
            dimension_semantics=("arbitrary",), vmem_limit_bytes=VMEM_LIMIT),
        name="proj_gmlp",
    )(x2, g_pre, w_in, gmat, lng, lnb, sw, sb, *later_weights)


def _t5_bucket_np(dist):
    max_exact = NUM_BUCKETS // 2
    d = np.maximum(dist, 1).astype(np.float64)
    large = max_exact + (np.log(d / max_exact) / math.log(MAX_DISTANCE / max_exact)
                         * (NUM_BUCKETS - max_exact))
    large = np.minimum(large.astype(np.int32), NUM_BUCKETS - 1)
    return np.where(dist < max_exact, dist, large)


def _bucket_starts(d):
    b = _t5_bucket_np(np.arange(KEYS_BACK + 1) * d)
    return [(r, int(b[r])) for r in range(KEYS_BACK + 1) if r == 0 or b[r] != b[r - 1]]


def _attn_kernel(rb_ref, q_ref, k_ref, v_ref, o_ref,
                 qs, kp, va, bt, stg, mb, lb, ab):
    seq = q_ref.shape[0]
    nblk = seq // KEYS_BACK
    padded = seq + KEYS_BACK
    assert DILATIONS == (1, 4, 16) and nblk == DILATIONS[-1]
    lane = lax.broadcasted_iota(jnp.int32, (KEYS_BACK, LANES), 1)
    first_head = lane < HEAD_DIM

    @pl.when(pl.program_id(1) == 0)
    def _():
        i = lax.broadcasted_iota(jnp.int32, (KEYS_BACK, 2 * KEYS_BACK), 0)
        j = lax.broadcasted_iota(jnp.int32, (KEYS_BACK, 2 * KEYS_BACK), 1)
        rel = KEYS_BACK + i - j
        band = (rel >= 0) & (rel <= KEYS_BACK)
        band_cur = band & (j >= KEYS_BACK)
        for di, d in enumerate(DILATIONS):
            for hh in range(HEADS_PER_STEP):
                head = pl.program_id(0) * HEADS_PER_STEP + hh
                val = jnp.zeros(rel.shape, F32)
                for start, bucket in _bucket_starts(d):
                    val = jnp.where(rel >= start, rb_ref[bucket, head], val)
                val = val * LOG2E
                rows = pl.ds(hh * KEYS_BACK, KEYS_BACK)
                bt[2 * di, rows, :] = jnp.where(band, val, NEG_INF)
                bt[2 * di + 1, rows, :] = jnp.where(band_cur, val, NEG_INF)

    d4 = DILATIONS[1]
    quarter = seq // d4

    def d4_rows(c):
        return pl.ds(c // d4 + (c % d4) * KEYS_BACK * d4, KEYS_BACK, stride=d4)

    def d16_rows_in_d4_order(r):
        return pl.ds((r % d4) * quarter + r // d4, KEYS_BACK, stride=d4)

    def put_block(j, key_row, qv, kv, vv):
        qs[j, 0:KEYS_BACK, :] = jnp.where(first_head, qv, 0.0).astype(BF16)
        qs[j, KEYS_BACK:, :] = jnp.where(first_head, 0.0, qv).astype(BF16)
        kp[pl.ds(key_row, KEYS_BACK), :] = kv.astype(BF16)
        va[pl.ds(key_row, KEYS_BACK), 0:LANES] = vv.astype(BF16)

    @pl.when((pl.program_id(0) == 0) & (pl.program_id(1) == 0))
    def _():
        for di in range(len(DILATIONS)):
            kp[pl.ds(di * padded, KEYS_BACK), :] = jnp.zeros((KEYS_BACK, LANES), BF16)
            va[pl.ds(di * padded, KEYS_BACK), :] = jnp.zeros((KEYS_BACK, 2 * LANES), BF16)
            va[pl.ds(di * padded + KEYS_BACK, seq), LANES:] = jnp.ones((seq, LANES), BF16)

    for g in range(nblk):
        rows = pl.ds(g * KEYS_BACK, KEYS_BACK)
        put_block(g, (g + 1) * KEYS_BACK, q_ref[rows, :], k_ref[rows, :], v_ref[rows, :])
    for g in range(nblk):
        src = d4_rows(g)
        rows = pl.ds(g * KEYS_BACK, KEYS_BACK)
        qv, kv, vv = q_ref[src, :], k_ref[src, :], v_ref[src, :]
        stg[0, rows, :] = qv
        stg[1, rows, :] = kv
        stg[2, rows, :] = vv
        put_block(nblk + g, padded + (g + 1) * KEYS_BACK, qv, kv, vv)
    for g in range(nblk):
        src = d16_rows_in_d4_order(g)
        put_block(2 * nblk + g, 2 * padded + (g + 1) * KEYS_BACK,
                  stg[0, src, :], stg[1, src, :], stg[2, src, :])

    def out_rows(j):
        if j // nblk == 2:
            r = j % nblk
            return pl.ds(2 * seq + (r % d4) * quarter + r // d4, KEYS_BACK, stride=d4)
        return pl.ds(j * KEYS_BACK, KEYS_BACK)

    def key_rows(j):
        return pl.ds((j // nblk) * padded + (j % nblk) * KEYS_BACK, 2 * KEYS_BACK)

    def qk(j):
        return _dot_nt(qs[j], kp[key_rows(j), :])

    def softmax(j, s):
        di, g = j // nblk, j % nblk
        no_prev = g % (nblk // DILATIONS[di]) == 0
        s = s + bt[2 * di + int(no_prev)]
        m = jnp.max(s, axis=-1, keepdims=True)
        mb[out_rows(j), :] = jnp.where(first_head, m[:KEYS_BACK], m[KEYS_BACK:])
        return jnp.exp2(s - m).astype(BF16)

    def pv(j, p):
        r = _dot(p, va[key_rows(j), :])
        ab[out_rows(j), :] = jnp.where(first_head, r[:KEYS_BACK, :LANES], r[KEYS_BACK:, :LANES])
        lb[out_rows(j), :] = jnp.where(first_head, r[:KEYS_BACK, LANES:], r[KEYS_BACK:, LANES:])

    n_blocks = len(DILATIONS) * nblk
    for j in range(n_blocks):
        pv(j, softmax(j, qk(j)))

    def combine(c, carry):
        tok = d4_rows(c)
        views = (tok,
                 pl.ds(pl.multiple_of(seq + c * KEYS_BACK, KEYS_BACK), KEYS_BACK),
                 pl.ds(pl.multiple_of(2 * seq + c * KEYS_BACK, KEYS_BACK), KEYS_BACK))
        ms = [mb[v, :] for v in views]
        mx = jnp.maximum(jnp.maximum(ms[0], ms[1]), ms[2])
        num = jnp.zeros((KEYS_BACK, LANES), F32)
        den = jnp.zeros((KEYS_BACK, LANES), F32)
        for m, v in zip(ms, views):
            e = jnp.exp2(m - mx)
            num = num + e * ab[v, :]
            den = den + e * lb[v, :]
        o_ref[tok, :] = num / den
        return carry

    lax.fori_loop(0, nblk, combine, 0)


def _attn_call(rel_bias, q, k, v, batch, seq):
    nd = len(DILATIONS)
    nblk = seq // KEYS_BACK
    n_pairs = B_HEADS // HEADS_PER_STEP
    io_spec = pl.BlockSpec((seq, LANES), lambda hp, b: (b, hp))
    return pl.pallas_call(
        _attn_kernel,
        grid=(n_pairs, batch),
        in_specs=[
            pl.BlockSpec(memory_space=pltpu.SMEM),
            io_spec, io_spec, io_spec,
        ],
        out_specs=io_spec,
        out_shape=jax.ShapeDtypeStruct((batch * seq, B_WIDTH), F32),
        scratch_shapes=[
            pltpu.VMEM((nd * nblk, 2 * KEYS_BACK, LANES), BF16),
            pltpu.VMEM((nd * (seq + KEYS_BACK), LANES), BF16),
            pltpu.VMEM((nd * (seq + KEYS_BACK), 2 * LANES), BF16),
            pltpu.VMEM((2 * nd, 2 * KEYS_BACK, 2 * KEYS_BACK), F32),
            pltpu.VMEM((3, seq, LANES), F32),
            pltpu.VMEM((nd * seq, LANES), F32),
            pltpu.VMEM((nd * seq, LANES), F32),
            pltpu.VMEM((nd * seq, LANES), F32),
        ],
        compiler_params=pltpu.CompilerParams(
            dimension_semantics=("arbitrary", "arbitrary"), vmem_limit_bytes=VMEM_LIMIT),
        name="dilated_attn",
    )(rel_bias, q, k, v)


def _ffn_kernel(x_ref, a_ref, b_ref, wo_ref, gpost_ref, gpre2_ref, wg_ref, wu_ref, cw_ref, cb_ref,
                wd_hbm, gpost2_ref, o_ref, gbuf, carry, act_buf, wd_ref, wd_stage, sems,
                *, tiles_per_seq):
    @pl.when(pl.program_id(0) == 0)
    def _():
        _load_weight_as_bf16(wd_hbm, wd_ref, wd_stage, sems)

    tm = x_ref.shape[0]
    halo = carry.shape[1]
    n_chunks = wd_ref.shape[0] // FF_CHUNK
    first = pl.program_id(0) % tiles_per_seq == 0

    @pl.when(first)
    def _():
        carry[...] = jnp.zeros(carry.shape, F32)

    y1 = (_dot(a_ref[...], wo_ref[0:A_WIDTH, :])
          + _dot(b_ref[...].astype(BF16), wo_ref[A_WIDTH:, :]))
    x1 = x_ref[...] + _rms(y1, gpost_ref[...])
    h2 = _rms(x1, gpre2_ref[...]).astype(BF16)

    def gate_up(c):
        cols = slice(c * FF_CHUNK, (c + 1) * FF_CHUNK)
        return _dot(h2, wg_ref[:, cols]), _dot(h2, wu_ref[:, cols])

    gu = gate_up(0)
    for c in range(n_chunks):
        cols = slice(c * FF_CHUNK, (c + 1) * FF_CHUNK)
        gu_next = gate_up(c + 1) if c + 1 < n_chunks else None
        g, up = gu
        gbuf[0:halo, :] = carry[c]
        gbuf[halo:halo + tm, :] = g
        carry[c] = g[tm - halo:, :]
        cw = cw_ref[:, cols]
        cv = (cb_ref[:, cols]
              + gbuf[halo - 2:halo - 2 + tm, :] * cw[0:1]
              + gbuf[halo - 1:halo - 1 + tm, :] * cw[1:2]
              + g * cw[2:3])
        act_buf[:, cols] = (jax.nn.gelu(cv) * up).astype(BF16)
        gu = gu_next
    o_ref[...] = x1 + _rms(_dot(act_buf[...], wd_ref[...]), gpost2_ref[...])


def _ffn_call(x2, a, b, wo, gpost, gpre2, wg, wu, cw, cb, wd, gpost2, seq):
    n, d = x2.shape
    tm = FFN_TOKENS_PER_STEP
    ff = wd.shape[0]
    halo = SUBLANES
    const = lambda i: (0, 0)
    tile = lambda w: pl.BlockSpec((tm, w), lambda i: (i, 0))
    once = pl.Buffered(1)
    return pl.pallas_call(
        functools.partial(_ffn_kernel, tiles_per_seq=seq // tm),
        grid=(n // tm,),
        in_specs=[
            tile(d), tile(A_WIDTH), tile(B_WIDTH),
            pl.BlockSpec((d, d), const, pipeline_mode=once),
            pl.BlockSpec((1, d), const),
            pl.BlockSpec((1, d), const),
            pl.BlockSpec((d, ff), const, pipeline_mode=once),
            pl.BlockSpec((d, ff), const, pipeline_mode=once),
            pl.BlockSpec((CONV_WIDTH, ff), const),
            pl.BlockSpec((1, ff), const),
            pl.BlockSpec(memory_space=pl.ANY),
            pl.BlockSpec((1, d), const),
        ],
        out_specs=tile(d),
        out_shape=jax.ShapeDtypeStruct((n, d), F32),
        scratch_shapes=[
            pltpu.VMEM((halo + tm, FF_CHUNK), F32),
            pltpu.VMEM((ff // FF_CHUNK, halo, FF_CHUNK), F32),
            pltpu.VMEM((tm, ff), BF16),
            pltpu.VMEM((ff, d), BF16),
            pltpu.VMEM((2, WEIGHT_ROWS_PER_COPY, d), F32),
            pltpu.SemaphoreType.DMA((2,)),
        ],
        compiler_params=pltpu.CompilerParams(
            dimension_semantics=("arbitrary",), vmem_limit_bytes=VMEM_LIMIT),
        name="outproj_ffn",
    )(x2, a, b, wo, gpost, gpre2, wg, wu, cw, cb, wd, gpost2)


def kernel(x, norm_mix_pre, norm_mix_post, norm_ffn_pre, norm_ffn_post, w_in, ln_v_gain, ln_v_bias,
           spatial_w, spatial_b, rel_bias, w_out, w_gate, w_up, conv_w, conv_b, w_down):
    batch, seq, d = x.shape
    depth = w_in.shape[0]
    ff = w_gate.shape[-1]
    assert seq % PROJ_TOKENS_PER_STEP == 0 and seq % FFN_TOKENS_PER_STEP == 0
    assert ff % FF_CHUNK == 0 and d == A_WIDTH + B_WIDTH
    assert seq // KEYS_BACK == max(DILATIONS)

    gmat = jnp.asarray(np.kron(np.eye(A_GROUPS), np.ones((HEAD_DIM, HEAD_DIM))), BF16)
    x2 = x.reshape(batch * seq, d)
    for l in range(depth):
        row = lambda p: p[l].reshape(1, -1).astype(F32)
        sb = jnp.repeat(spatial_b[l].astype(F32).T, HEAD_DIM, axis=1)
        later = [w[l].astype(F32) for w in (w_out, w_gate, w_up)]
        a_out, q, k, v, wo_bf, wg_bf, wu_bf = _proj_call(
            x2, row(norm_mix_pre), w_in[l].astype(F32), gmat,
            row(ln_v_gain), row(ln_v_bias), spatial_w[l].astype(F32), sb, later)
        b_out = _attn_call(rel_bias.astype(F32), q, k, v, batch, seq)
        x2 = _ffn_call(
            x2, a_out, b_out, wo_bf, row(norm_mix_post), row(norm_ffn_pre),
            wg_bf, wu_bf, conv_w[l].astype(F32), row(conv_b),
            w_down[l].astype(F32), row(norm_ffn_post), seq)
    return x2.reshape(batch, seq, d)
```

```python
import functools
import math

import numpy as np
import jax
import jax.numpy as jnp
from jax import lax
from jax.experimental import pallas as pl
from jax.experimental.pallas import tpu as pltpu

F32 = jnp.float32
BF16 = jnp.bfloat16

HEAD_DIM = 64
A_GROUPS = 4
A_WIDTH = A_GROUPS * HEAD_DIM
B_HEADS = 12
B_WIDTH = B_HEADS * HEAD_DIM
CHUNK = 128
DILATIONS = (1, 4, 16)
KEYS_BACK = 128
NUM_BUCKETS = 32
MAX_DISTANCE = 2048
CONV_WIDTH = 3
NORM_EPS = 1e-6
NEG_INF = -1e30
LOG2E = 1.4426950408889634

LANES = 128
SUBLANES = 8
HEADS_PER_STEP = LANES // HEAD_DIM
FF_CHUNK = 256
PROJ_TOKENS_PER_STEP = 1024
ROWS_PER_PASS = 512
FFN_TOKENS_PER_STEP = 512
WEIGHT_ROWS_PER_COPY = 256
N_LATER_WEIGHTS = 4
VMEM_LIMIT = 56 * 1024 * 1024


def _rms(x, g):
    ms = jnp.mean(x * x, axis=-1, keepdims=True)
    return x * lax.rsqrt(ms + NORM_EPS) * g


def _dot(a, b):
    return jnp.dot(a, b, preferred_element_type=F32)


def _dot_nt(a, b):
    return lax.dot_general(a, b, (((1,), (1,)), ((), ())), preferred_element_type=F32)


def _load_weight_as_bf16(w_hbm, dst, stage, sems):
    rows = stage.shape[1]
    n_chunks = w_hbm.shape[0] // rows

    def chunk_copy(c):
        return pltpu.make_async_copy(
            w_hbm.at[pl.ds(c * rows, rows), :], stage.at[c % 2], sems.at[c % 2])

    chunk_copy(0).start()
    for c in range(n_chunks):
        if c + 1 < n_chunks:
            chunk_copy(c + 1).start()
        chunk_copy(c).wait()
        dst[pl.ds(c * rows, rows), :] = stage[c % 2].astype(BF16)


def _proj_kernel(x_ref, g_ref, w_hbm, gmat_ref, lng_ref, lnb_ref, sw_ref, sb_ref, *rest, wd_col_steps):
    later_f32, rest = rest[:N_LATER_WEIGHTS], rest[N_LATER_WEIGHTS:]
    a_ref, q_ref, k_ref, v_ref = rest[:4]
    later_bf16, (uv_ref, w_ref, w_stage, w_sems) = rest[4:4 + N_LATER_WEIGHTS], rest[4 + N_LATER_WEIGHTS:]

    @pl.when(pl.program_id(0) == 0)
    def _():
        _load_weight_as_bf16(w_hbm, w_ref, w_stage, w_sems)

    for src, dst in zip(later_f32[:-1], later_bf16[:-1]):
        dst[...] = src[...].astype(BF16)

    @pl.when(pl.program_id(0) < wd_col_steps)
    def _():
        later_bf16[-1][...] = later_f32[-1][...].astype(BF16)

    gmat = gmat_ref[...]
    row = lax.broadcasted_iota(jnp.int32, (CHUNK, CHUNK), 0)
    col = lax.broadcasted_iota(jnp.int32, (CHUNK, CHUNK), 1)
    tril = row >= col
    ws = [jnp.where(tril, sw_ref[g], 0.0).astype(BF16) for g in range(A_GROUPS)]
    grp = lax.broadcasted_iota(jnp.int32, (CHUNK, A_WIDTH), 1) // HEAD_DIM
    sb = sb_ref[...]
    o = 2 * A_WIDTH

    def group_sum(t):
        hi = t.astype(BF16)
        lo = (t - hi.astype(F32)).astype(BF16)
        return _dot(hi, gmat) + _dot(lo, gmat)

    n_pass = x_ref.shape[0] // ROWS_PER_PASS
    pass_rows = [pl.ds(p * ROWS_PER_PASS, ROWS_PER_PASS) for p in range(n_pass)]

    for rows in pass_rows:
        h = _rms(x_ref[rows, :], g_ref[...]).astype(BF16)
        q_ref[rows, :] = _dot(h, w_ref[:, o:o + B_WIDTH]) * (LOG2E / math.sqrt(HEAD_DIM))
        k_ref[rows, :] = _dot(h, w_ref[:, o + B_WIDTH:o + 2 * B_WIDTH])
        v_ref[rows, :] = _dot(h, w_ref[:, o + 2 * B_WIDTH:o + 3 * B_WIDTH])
        uv_ref[rows, :] = _dot(h, w_ref[:, 0:o])

    for rows in pass_rows:
        u = jax.nn.gelu(uv_ref[rows, 0:A_WIDTH])
        vf = jax.nn.gelu(uv_ref[rows, A_WIDTH:o])
        mu = group_sum(vf) * (1.0 / HEAD_DIM)
        dl = vf - mu
        var = group_sum(dl * dl) * (1.0 / HEAD_DIM)
        vn = (dl * lax.rsqrt(var + NORM_EPS) * lng_ref[...] + lnb_ref[...]).astype(BF16)

        for c in range(ROWS_PER_PASS // CHUNK):
            chunk = slice(c * CHUNK, (c + 1) * CHUNK)
            vc = vn[chunk]
            z = _dot(ws[0], vc)
            for g in range(1, A_GROUPS):
                z = jnp.where(grp == g, _dot(ws[g], vc), z)
            a_ref[pl.ds(rows.start + c * CHUNK, CHUNK), :] = (u[chunk] * (z + sb)).astype(BF16)


def _proj_call(x2, g_pre, w_in, gmat, lng, lnb, sw, sb, later_weights):
    n, d = x2.shape
    tm = PROJ_TOKENS_PER_STEP
    cols = w_in.shape[1]
    steps = n // tm
    const = lambda i: (0, 0)
    once = pl.Buffered(1)
    assert len(later_weights) == N_LATER_WEIGHTS
    *by_rows, w_down = later_weights
    slices = [pl.BlockSpec((w.shape[0] // steps, w.shape[1]), lambda i: (i, 0)) for w in by_rows]
    assert all(w.shape[0] % (steps * 2 * SUBLANES) == 0 for w in by_rows)
    wd_col_steps = w_down.shape[1] // LANES
    assert wd_col_steps <= steps
    slices.append(pl.BlockSpec((w_down.shape[0], LANES),
                               lambda i: (0, jnp.minimum(i, wd_col_steps - 1))))
    return pl.pallas_call(
        functools.partial(_proj_kernel, wd_col_steps=wd_col_steps),
        grid=(n // tm,),
        in_specs=[
            pl.BlockSpec((tm, d), lambda i: (i, 0)),
            pl.BlockSpec((1, d), const, pipeline_mode=once),
            pl.BlockSpec(memory_space=pl.ANY),
            pl.BlockSpec((A_WIDTH, A_WIDTH), const, pipeline_mode=once),
            pl.BlockSpec((1, A_WIDTH), const, pipeline_mode=once),
            pl.BlockSpec((1, A_WIDTH), const, pipeline_mode=once),
            pl.BlockSpec((A_GROUPS, CHUNK, CHUNK), lambda i: (0, 0, 0), pipeline_mode=once),
            pl.BlockSpec((CHUNK, A_WIDTH), const, pipeline_mode=once),
        ] + slices,
        out_specs=[
            pl.BlockSpec((tm, A_WIDTH), lambda i: (i, 0)),
            pl.BlockSpec((tm, B_WIDTH), lambda i: (i, 0)),
            pl.BlockSpec((tm, B_WIDTH), lambda i: (i, 0)),
            pl.BlockSpec((tm, B_WIDTH), lambda i: (i, 0)),
        ] + slices,
        out_shape=[
            jax.ShapeDtypeStruct((n, A_WIDTH), BF16),
            jax.ShapeDtypeStruct((n, B_WIDTH), F32),
            jax.ShapeDtypeStruct((n, B_WIDTH), F32),
            jax.ShapeDtypeStruct((n, B_WIDTH), F32),
        ] + [jax.ShapeDtypeStruct(w.shape, BF16) for w in later_weights],
        scratch_shapes=[
            pltpu.VMEM((tm, 2 * A_WIDTH), F32),
            pltpu.VMEM((d, cols), BF16),
            pltpu.VMEM((2, WEIGHT_ROWS_PER_COPY, cols), F32),
            pltpu.SemaphoreType.DMA((2,)),
        ],
        compiler_params=pltpu.CompilerParams(
            dimension_semantics=("arbitrary",), vmem_limit_bytes=VMEM_LIMIT),
        name="proj_gmlp",
    )(x2, g_pre, w_in, gmat, lng, lnb, sw, sb, *later_weights)


def _t5_bucket_np(dist):
    max_exact = NUM_BUCKETS // 2
    d = np.maximum(dist, 1).astype(np.float64)
    large = max_exact + (np.log(d / max_exact) / math.log(MAX_DISTANCE / max_exact)
                         * (NUM_BUCKETS - max_exact))
    large = np.minimum(large.astype(np.int32), NUM_BUCKETS - 1)
    return np.where(dist < max_exact, dist, large)


def _bucket_starts(d):
    b = _t5_bucket_np(np.arange(KEYS_BACK + 1) * d)
    return [(r, int(b[r])) for r in range(KEYS_BACK + 1) if r == 0 or b[r] != b[r - 1]]


def _attn_kernel(rb_ref, q_ref, k_ref, v_ref, o_ref,
                 qs, kp, va, bt, stg, mb, lb, ab):
    seq = q_ref.shape[0]
    nblk = seq // KEYS_BACK
    padded = seq + KEYS_BACK
    assert DILATIONS == (1, 4, 16) and nblk == DILATIONS[-1]
    lane = lax.broadcasted_iota(jnp.int32, (KEYS_BACK, LANES), 1)
    first_head = lane < HEAD_DIM

    @pl.when(pl.program_id(1) == 0)
    def _():
        i = lax.broadcasted_iota(jnp.int32, (KEYS_BACK, 2 * KEYS_BACK), 0)
        j = lax.broadcasted_iota(jnp.int32, (KEYS_BACK, 2 * KEYS_BACK), 1)
        rel = KEYS_BACK + i - j
        band = (rel >= 0) & (rel <= KEYS_BACK)
        band_cur = band & (j >= KEYS_BACK)
        for di, d in enumerate(DILATIONS):
            for hh in range(HEADS_PER_STEP):
                head = pl.program_id(0) * HEADS_PER_STEP + hh
                val = jnp.zeros(rel.shape, F32)
                for start, bucket in _bucket_starts(d):
                    val = jnp.where(rel >= start, rb_ref[bucket, head], val)
                val = val * LOG2E
                rows = pl.ds(hh * KEYS_BACK, KEYS_BACK)
                bt[2 * di, rows, :] = jnp.where(band, val, NEG_INF)
                bt[2 * di + 1, rows, :] = jnp.where(band_cur, val, NEG_INF)

    d4 = DILATIONS[1]
    quarter = seq // d4

    def d4_rows(c):
        return pl.ds(c // d4 + (c % d4) * KEYS_BACK * d4, KEYS_BACK, stride=d4)

    def d16_rows_in_d4_order(r):
        return pl.ds((r % d4) * quarter + r // d4, KEYS_BACK, stride=d4)

    def put_block(j, key_row, qv, kv, vv):
        qs[j, 0:KEYS_BACK, :] = jnp.where(first_head, qv, 0.0).astype(BF16)
        qs[j, KEYS_BACK:, :] = jnp.where(first_head, 0.0, qv).astype(BF16)
        kp[pl.ds(key_row, KEYS_BACK), :] = kv.astype(BF16)
        va[pl.ds(key_row, KEYS_BACK), 0:LANES] = vv.astype(BF16)

    @pl.when((pl.program_id(0) == 0) & (pl.program_id(1) == 0))
    def _():
        for di in range(len(DILATIONS)):
            kp[pl.ds(di * padded, KEYS_BACK), :] = jnp.zeros((KEYS_BACK, LANES), BF16)
            va[pl.ds(di * padded, KEYS_BACK), :] = jnp.zeros((KEYS_BACK, 2 * LANES), BF16)
            va[pl.ds(di * padded + KEYS_BACK, seq), LANES:] = jnp.ones((seq, LANES), BF16)

    for g in range(nblk):
        rows = pl.ds(g * KEYS_BACK, KEYS_BACK)
        put_block(g, (g + 1) * KEYS_BACK, q_ref[rows, :], k_ref[rows, :], v_ref[rows, :])
    for g in range(nblk):
        src = d4_rows(g)
        rows = pl.ds(g * KEYS_BACK, KEYS_BACK)
        qv, kv, vv = q_ref[src, :], k_ref[src, :], v_ref[src, :]
        stg[0, rows, :] = qv
        stg[1, rows, :] = kv
        stg[2, rows, :] = vv
        put_block(nblk + g, padded + (g + 1) * KEYS_BACK, qv, kv, vv)
    for g in range(nblk):
        src = d16_rows_in_d4_order(g)
        put_block(2 * nblk + g, 2 * padded + (g + 1) * KEYS_BACK,
                  stg[0, src, :], stg[1, src, :], stg[2, src, :])

    def out_rows(j):
        if j // nblk == 2:
            r = j % nblk
            return pl.ds(2 * seq + (r % d4) * quarter + r // d4, KEYS_BACK, stride=d4)
        return pl.ds(j * KEYS_BACK, KEYS_BACK)

    def key_rows(j):
        return pl.ds((j // nblk) * padded + (j % nblk) * KEYS_BACK, 2 * KEYS_BACK)

    def qk(j):
        return _dot_nt(qs[j], kp[key_rows(j), :])

    def softmax(j, s):
        di, g = j // nblk, j % nblk
        no_prev = g % (nblk // DILATIONS[di]) == 0
        s = s + bt[2 * di + int(no_prev)]
        m = jnp.max(s, axis=-1, keepdims=True)
        mb[out_rows(j), :] = jnp.where(first_head, m[:KEYS_BACK], m[KEYS_BACK:])
        return jnp.exp2(s - m).astype(BF16)

    def pv(j, p):
        r = _dot(p, va[key_rows(j), :])
        ab[out_rows(j), :] = jnp.where(first_head, r[:KEYS_BACK, :LANES], r[KEYS_BACK:, :LANES])
        lb[out_rows(j), :] = jnp.where(first_head, r[:KEYS_BACK, LANES:], r[KEYS_BACK:, LANES:])

    n_blocks = len(DILATIONS) * nblk
    for j in range(n_blocks):
        pv(j, softmax(j, qk(j)))

    def combine(c, carry):
        tok = d4_rows(c)
        views = (tok,
                 pl.ds(pl.multiple_of(seq + c * KEYS_BACK, KEYS_BACK), KEYS_BACK),
                 pl.ds(pl.multiple_of(2 * seq + c * KEYS_BACK, KEYS_BACK), KEYS_BACK))
        ms = [mb[v, :] for v in views]
        mx = jnp.maximum(jnp.maximum(ms[0], ms[1]), ms[2])
        num = jnp.zeros((KEYS_BACK, LANES), F32)
        den = jnp.zeros((KEYS_BACK, LANES), F32)
        for m, v in zip(ms, views):
            e = jnp.exp2(m - mx)
            num = num + e * ab[v, :]
            den = den + e * lb[v, :]
        o_ref[tok, :] = num / den
        return carry

    lax.fori_loop(0, nblk, combine, 0)


def _attn_call(rel_bias, q, k, v, batch, seq):
    nd = len(DILATIONS)
    nblk = seq // KEYS_BACK
    n_pairs = B_HEADS // HEADS_PER_STEP
    io_spec = pl.BlockSpec((seq, LANES), lambda hp, b: (b, hp))
    return pl.pallas_call(
        _attn_kernel,
        grid=(n_pairs, batch),
        in_specs=[
            pl.BlockSpec(memory_space=pltpu.SMEM),
            io_spec, io_spec, io_spec,
        ],
        out_specs=io_spec,
        out_shape=jax.ShapeDtypeStruct((batch * seq, B_WIDTH), F32),
        scratch_shapes=[
            pltpu.VMEM((nd * nblk, 2 * KEYS_BACK, LANES), BF16),
            pltpu.VMEM((nd * (seq + KEYS_BACK), LANES), BF16),
            pltpu.VMEM((nd * (seq + KEYS_BACK), 2 * LANES), BF16),
            pltpu.VMEM((2 * nd, 2 * KEYS_BACK, 2 * KEYS_BACK), F32),
            pltpu.VMEM((3, seq, LANES), F32),
            pltpu.VMEM((nd * seq, LANES), F32),
            pltpu.VMEM((nd * seq, LANES), F32),
            pltpu.VMEM((nd * seq, LANES), F32),
        ],
        compiler_params=pltpu.CompilerParams(
            dimension_semantics=("arbitrary", "arbitrary"), vmem_limit_bytes=VMEM_LIMIT),
        name="dilated_attn",
    )(rel_bias, q, k, v)


def _ffn_kernel(x_ref, a_ref, b_ref, wo_ref, gpost_ref, gpre2_ref, wg_ref, wu_ref, cw_ref, cb_ref,
                wd_ref, gpost2_ref, o_ref, gbuf, carry, act_buf, *, tiles_per_seq):
    tm = x_ref.shape[0]
    halo = carry.shape[1]
    n_chunks = wd_ref.shape[0] // FF_CHUNK
    first = pl.program_id(0) % tiles_per_seq == 0

    @pl.when(first)
    def _():
        carry[...] = jnp.zeros(carry.shape, F32)

    y1 = (_dot(a_ref[...], wo_ref[0:A_WIDTH, :])
          + _dot(b_ref[...].astype(BF16), wo_ref[A_WIDTH:, :]))
    x1 = x_ref[...] + _rms(y1, gpost_ref[...])
    h2 = _rms(x1, gpre2_ref[...]).astype(BF16)

    def gate_up(c):
        cols = slice(c * FF_CHUNK, (c + 1) * FF_CHUNK)
        return _dot(h2, wg_ref[:, cols]), _dot(h2, wu_ref[:, cols])

    gu = gate_up(0)
    for c in range(n_chunks):
        cols = slice(c * FF_CHUNK, (c + 1) * FF_CHUNK)
        gu_next = gate_up(c + 1) if c + 1 < n_chunks else None
        g, up = gu
        gbuf[0:halo, :] = carry[c]
        gbuf[halo:halo + tm, :] = g
        carry[c] = g[tm - halo:, :]
        cw = cw_ref[:, cols]
        cv = (cb_ref[:, cols]
              + gbuf[halo - 2:halo - 2 + tm, :] * cw[0:1]
              + gbuf[halo - 1:halo - 1 + tm, :] * cw[1:2]
              + g * cw[2:3])
        act_buf[:, cols] = (jax.nn.gelu(cv) * up).astype(BF16)
        gu = gu_next
    o_ref[...] = x1 + _rms(_dot(act_buf[...], wd_ref[...]), gpost2_ref[...])


def _ffn_call(x2, a, b, wo, gpost, gpre2, wg, wu, cw, cb, wd, gpost2, seq):
    n, d = x2.shape
    tm = FFN_TOKENS_PER_STEP
    ff = wd.shape[0]
    halo = SUBLANES
    const = lambda i: (0, 0)
    tile = lambda w: pl.BlockSpec((tm, w), lambda i: (i, 0))
    once = pl.Buffered(1)
    return pl.pallas_call(
        functools.partial(_ffn_kernel, tiles_per_seq=seq // tm),
        grid=(n // tm,),
        in_specs=[
            tile(d), tile(A_WIDTH), tile(B_WIDTH),
            pl.BlockSpec((d, d), const, pipeline_mode=once),
            pl.BlockSpec((1, d), const),
            pl.BlockSpec((1, d), const),
            pl.BlockSpec((d, ff), const, pipeline_mode=once),
            pl.BlockSpec((d, ff), const, pipeline_mode=once),
            pl.BlockSpec((CONV_WIDTH, ff), const),
            pl.BlockSpec((1, ff), const),
            pl.BlockSpec((ff, d), const, pipeline_mode=once),
            pl.BlockSpec((1, d), const),
        ],
        out_specs=tile(d),
        out_shape=jax.ShapeDtypeStruct((n, d), F32),
        scratch_shapes=[
            pltpu.VMEM((halo + tm, FF_CHUNK), F32),
            pltpu.VMEM((ff // FF_CHUNK, halo, FF_CHUNK), F32),
            pltpu.VMEM((tm, ff), BF16),
        ],
        compiler_params=pltpu.CompilerParams(
            dimension_semantics=("arbitrary",), vmem_limit_bytes=VMEM_LIMIT),
        name="outproj_ffn",
    )(x2, a, b, wo, gpost, gpre2, wg, wu, cw, cb, wd, gpost2)


def kernel(x, norm_mix_pre, norm_mix_post, norm_ffn_pre, norm_ffn_post, w_in, ln_v_gain, ln_v_bias,
           spatial_w, spatial_b, rel_bias, w_out, w_gate, w_up, conv_w, conv_b, w_down):
    batch, seq, d = x.shape
    depth = w_in.shape[0]
    ff = w_gate.shape[-1]
    assert seq % PROJ_TOKENS_PER_STEP == 0 and seq % FFN_TOKENS_PER_STEP == 0
    assert ff % FF_CHUNK == 0 and d == A_WIDTH + B_WIDTH
    assert seq // KEYS_BACK == max(DILATIONS)

    gmat = jnp.asarray(np.kron(np.eye(A_GROUPS), np.ones((HEAD_DIM, HEAD_DIM))), BF16)
    x2 = x.reshape(batch * seq, d)
    for l in range(depth):
        row = lambda p: p[l].reshape(1, -1).astype(F32)
        sb = jnp.repeat(spatial_b[l].astype(F32).T, HEAD_DIM, axis=1)
        later = [w[l].astype(F32) for w in (w_out, w_gate, w_up, w_down)]
        a_out, q, k, v, wo_bf, wg_bf, wu_bf, wd_bf = _proj_call(
            x2, row(norm_mix_pre), w_in[l].astype(F32), gmat,
            row(ln_v_gain), row(ln_v_bias), spatial_w[l].astype(F32), sb, later)
        b_out = _attn_call(rel_bias.astype(F32), q, k, v, batch, seq)
        x2 = _ffn_call(
            x2, a_out, b_out, wo_bf, row(norm_mix_post), row(norm_ffn_pre),
            wg_bf, wu_bf, conv_w[l].astype(F32), row(conv_b),
            wd_bf, row(norm_ffn_post), seq)
    return x2.reshape(batch, seq, d)
```

```python
import functools
import math

import numpy as np
import jax
import jax.numpy as jnp
from jax import lax
from jax.experimental import pallas as pl
from jax.experimental.pallas import tpu as pltpu

F32 = jnp.float32
BF16 = jnp.bfloat16

HEAD_DIM = 64
A_GROUPS = 4
A_WIDTH = A_GROUPS * HEAD_DIM
B_HEADS = 12
B_WIDTH = B_HEADS * HEAD_DIM
CHUNK = 128
DILATIONS = (1, 4, 16)
KEYS_BACK = 128
NUM_BUCKETS = 32
MAX_DISTANCE = 2048
CONV_WIDTH = 3
NORM_EPS = 1e-6
NEG_INF = -1e30
LOG2E = 1.4426950408889634

LANES = 128
SUBLANES = 8
HEADS_PER_STEP = LANES // HEAD_DIM
FF_CHUNK = 256
PROJ_TOKENS_PER_STEP = 1024
ROWS_PER_PASS = 512
FFN_TOKENS_PER_STEP = 512
WEIGHT_ROWS_PER_COPY = 256
N_LATER_WEIGHTS = 4
VMEM_LIMIT = 56 * 1024 * 1024


def _rms(x, g):
    ms = jnp.mean(x * x, axis=-1, keepdims=True)
    return x * lax.rsqrt(ms + NORM_EPS) * g


def _dot(a, b):
    return jnp.dot(a, b, preferred_element_type=F32)


def _dot_nt(a, b):
    return lax.dot_general(a, b, (((1,), (1,)), ((), ())), preferred_element_type=F32)


def _load_weight_as_bf16(w_hbm, dst, stage, sems):
    rows = stage.shape[1]
    n_chunks = w_hbm.shape[0] // rows

    def chunk_copy(c):
        return pltpu.make_async_copy(
            w_hbm.at[pl.ds(c * rows, rows), :], stage.at[c % 2], sems.at[c % 2])

    chunk_copy(0).start()
    for c in range(n_chunks):
        if c + 1 < n_chunks:
            chunk_copy(c + 1).start()
        chunk_copy(c).wait()
        dst[pl.ds(c * rows, rows), :] = stage[c % 2].astype(BF16)


def _proj_kernel(x_ref, g_ref, w_hbm, gmat_ref, lng_ref, lnb_ref, sw_ref, sb_ref, *rest, wd_col_steps):
    later_f32, rest = rest[:N_LATER_WEIGHTS], rest[N_LATER_WEIGHTS:]
    a_ref, q_ref, k_ref, v_ref = rest[:4]
    later_bf16, (uv_ref, w_ref, w_stage, w_sems) = rest[4:4 + N_LATER_WEIGHTS], rest[4 + N_LATER_WEIGHTS:]

    @pl.when(pl.program_id(0) == 0)
    def _():
        _load_weight_as_bf16(w_hbm, w_ref, w_stage, w_sems)

    for src, dst in zip(later_f32[:-1], later_bf16[:-1]):
        dst[...] = src[...].astype(BF16)

    @pl.when(pl.program_id(0) < wd_col_steps)
    def _():
        later_bf16[-1][...] = later_f32[-1][...].astype(BF16)

    gmat = gmat_ref[...]
    row = lax.broadcasted_iota(jnp.int32, (CHUNK, CHUNK), 0)
    col = lax.broadcasted_iota(jnp.int32, (CHUNK, CHUNK), 1)
    tril = row >= col
    ws = [jnp.where(tril, sw_ref[g], 0.0).astype(BF16) for g in range(A_GROUPS)]
    grp2 = (lax.broadcasted_iota(jnp.int32, (CHUNK, 2 * A_WIDTH), 1) // HEAD_DIM) % A_GROUPS
    sb = sb_ref[...]
    o = 2 * A_WIDTH

    def group_sum(t):
        hi = t.astype(BF16)
        lo = (t - hi.astype(F32)).astype(BF16)
        return _dot(hi, gmat) + _dot(lo, gmat)

    n_pass = x_ref.shape[0] // ROWS_PER_PASS
    pass_rows = [pl.ds(p * ROWS_PER_PASS, ROWS_PER_PASS) for p in range(n_pass)]

    for rows in pass_rows:
        h = _rms(x_ref[rows, :], g_ref[...]).astype(BF16)
        q_ref[rows, :] = _dot(h, w_ref[:, o:o + B_WIDTH]) * (LOG2E / math.sqrt(HEAD_DIM))
        k_ref[rows, :] = _dot(h, w_ref[:, o + B_WIDTH:o + 2 * B_WIDTH])
        v_ref[rows, :] = _dot(h, w_ref[:, o + 2 * B_WIDTH:o + 3 * B_WIDTH])
        uv_ref[rows, :] = _dot(h, w_ref[:, 0:o])

    u = jax.nn.gelu(uv_ref[:, 0:A_WIDTH])
    vf = jax.nn.gelu(uv_ref[:, A_WIDTH:o])
    mu = group_sum(vf) * (1.0 / HEAD_DIM)
    dl = vf - mu
    var = group_sum(dl * dl) * (1.0 / HEAD_DIM)
    vn = (dl * lax.rsqrt(var + NORM_EPS) * lng_ref[...] + lnb_ref[...]).astype(BF16)

    for c in range(0, x_ref.shape[0] // CHUNK, 2):
        lo_rows = slice(c * CHUNK, (c + 1) * CHUNK)
        hi_rows = slice((c + 1) * CHUNK, (c + 2) * CHUNK)
        vc = jnp.concatenate([vn[lo_rows], vn[hi_rows]], axis=1)
        z = _dot(ws[0], vc)
        for g in range(1, A_GROUPS):
            z = jnp.where(grp2 == g, _dot(ws[g], vc), z)
        for half, chunk in enumerate((lo_rows, hi_rows)):
            zc = z[:, half * A_WIDTH:(half + 1) * A_WIDTH]
            a_ref[chunk, :] = (u[chunk] * (zc + sb)).astype(BF16)


def _proj_call(x2, g_pre, w_in, gmat, lng, lnb, sw, sb, later_weights):
    n, d = x2.shape
    tm = PROJ_TOKENS_PER_STEP
    cols = w_in.shape[1]
    steps = n // tm
    const = lambda i: (0, 0)
    once = pl.Buffered(1)
    assert len(later_weights) == N_LATER_WEIGHTS
    *by_rows, w_down = later_weights
    slices = [pl.BlockSpec((w.shape[0] // steps, w.shape[1]), lambda i: (i, 0)) for w in by_rows]
    assert all(w.shape[0] % (steps * 2 * SUBLANES) == 0 for w in by_rows)
    wd_col_steps = w_down.shape[1] // LANES
    assert wd_col_steps <= steps
    slices.append(pl.BlockSpec((w_down.shape[0], LANES),
                               lambda i: (0, jnp.minimum(i, wd_col_steps - 1))))
    return pl.pallas_call(
        functools.partial(_proj_kernel, wd_col_steps=wd_col_steps),
        grid=(n // tm,),
        in_specs=[
            pl.BlockSpec((tm, d), lambda i: (i, 0)),
            pl.BlockSpec((1, d), const, pipeline_mode=once),
            pl.BlockSpec(memory_space=pl.ANY),
            pl.BlockSpec((A_WIDTH, A_WIDTH), const, pipeline_mode=once),
            pl.BlockSpec((1, A_WIDTH), const, pipeline_mode=once),
            pl.BlockSpec((1, A_WIDTH), const, pipeline_mode=once),
            pl.BlockSpec((A_GROUPS, CHUNK, CHUNK), lambda i: (0, 0, 0), pipeline_mode=once),
            pl.BlockSpec((CHUNK, A_WIDTH), const, pipeline_mode=once),
        ] + slices,
        out_specs=[
            pl.BlockSpec((tm, A_WIDTH), lambda i: (i, 0)),
            pl.BlockSpec((tm, B_WIDTH), lambda i: (i, 0)),
            pl.BlockSpec((tm, B_WIDTH), lambda i: (i, 0)),
            pl.BlockSpec((tm, B_WIDTH), lambda i: (i, 0)),
        ] + slices,
        out_shape=[
            jax.ShapeDtypeStruct((n, A_WIDTH), BF16),
            jax.ShapeDtypeStruct((n, B_WIDTH), F32),
            jax.ShapeDtypeStruct((n, B_WIDTH), F32),
            jax.ShapeDtypeStruct((n, B_WIDTH), F32),
        ] + [jax.ShapeDtypeStruct(w.shape, BF16) for w in later_weights],
        scratch_shapes=[
            pltpu.VMEM((tm, 2 * A_WIDTH), F32),
            pltpu.VMEM((d, cols), BF16),
            pltpu.VMEM((2, WEIGHT_ROWS_PER_COPY, cols), F32),
            pltpu.SemaphoreType.DMA((2,)),
        ],
        compiler_params=pltpu.CompilerParams(
            dimension_semantics=("arbitrary",), vmem_limit_bytes=VMEM_LIMIT),
        name="proj_gmlp",
    )(x2, g_pre, w_in, gmat, lng, lnb, sw, sb, *later_weights)


def _t5_bucket_np(dist):
    max_exact = NUM_BUCKETS // 2
    d = np.maximum(dist, 1).astype(np.float64)
    large = max_exact + (np.log(d / max_exact) / math.log(MAX_DISTANCE / max_exact)
                         * (NUM_BUCKETS - max_exact))
    large = np.minimum(large.astype(np.int32), NUM_BUCKETS - 1)
    return np.where(dist < max_exact, dist, large)


def _bucket_starts(d):
    b = _t5_bucket_np(np.arange(KEYS_BACK + 1) * d)
    return [(r, int(b[r])) for r in range(KEYS_BACK + 1) if r == 0 or b[r] != b[r - 1]]


def _attn_kernel(rb_ref, q_ref, k_ref, v_ref, o_ref,
                 qs, kp, va, bt, stg, mb, lb, ab):
    seq = q_ref.shape[0]
    nblk = seq // KEYS_BACK
    padded = seq + KEYS_BACK
    assert DILATIONS == (1, 4, 16) and nblk == DILATIONS[-1]
    lane = lax.broadcasted_iota(jnp.int32, (KEYS_BACK, LANES), 1)
    first_head = lane < HEAD_DIM

    @pl.when(pl.program_id(1) == 0)
    def _():
        i = lax.broadcasted_iota(jnp.int32, (KEYS_BACK, 2 * KEYS_BACK), 0)
        j = lax.broadcasted_iota(jnp.int32, (KEYS_BACK, 2 * KEYS_BACK), 1)
        rel = KEYS_BACK + i - j
        band = (rel >= 0) & (rel <= KEYS_BACK)
        band_cur = band & (j >= KEYS_BACK)
        for di, d in enumerate(DILATIONS):
            for hh in range(HEADS_PER_STEP):
                head = pl.program_id(0) * HEADS_PER_STEP + hh
                val = jnp.zeros(rel.shape, F32)
                for start, bucket in _bucket_starts(d):
                    val = jnp.where(rel >= start, rb_ref[bucket, head], val)
                val = val * LOG2E
                rows = pl.ds(hh * KEYS_BACK, KEYS_BACK)
                bt[2 * di, rows, :] = jnp.where(band, val, NEG_INF)
                bt[2 * di + 1, rows, :] = jnp.where(band_cur, val, NEG_INF)

    d4 = DILATIONS[1]
    quarter = seq // d4

    def d4_rows(c):
        return pl.ds(c // d4 + (c % d4) * KEYS_BACK * d4, KEYS_BACK, stride=d4)

    def d16_rows_in_d4_order(r):
        return pl.ds((r % d4) * quarter + r // d4, KEYS_BACK, stride=d4)

    def put_block(j, key_row, qv, kv, vv):
        qs[j, 0:KEYS_BACK, :] = jnp.where(first_head, qv, 0.0).astype(BF16)
        qs[j, KEYS_BACK:, :] = jnp.where(first_head, 0.0, qv).astype(BF16)
        kp[pl.ds(key_row, KEYS_BACK), :] = kv.astype(BF16)
        va[pl.ds(key_row, KEYS_BACK), 0:LANES] = vv.astype(BF16)

    @pl.when((pl.program_id(0) == 0) & (pl.program_id(1) == 0))
    def _():
        for di in range(len(DILATIONS)):
            kp[pl.ds(di * padded, KEYS_BACK), :] = jnp.zeros((KEYS_BACK, LANES), BF16)
            va[pl.ds(di * padded, KEYS_BACK), :] = jnp.zeros((KEYS_BACK, 2 * LANES), BF16)
            va[pl.ds(di * padded + KEYS_BACK, seq), LANES:] = jnp.ones((seq, LANES), BF16)

    for g in range(nblk):
        rows = pl.ds(g * KEYS_BACK, KEYS_BACK)
        put_block(g, (g + 1) * KEYS_BACK, q_ref[rows, :], k_ref[rows, :], v_ref[rows, :])
    for g in range(nblk):
        src = d4_rows(g)
        rows = pl.ds(g * KEYS_BACK, KEYS_BACK)
        qv, kv, vv = q_ref[src, :], k_ref[src, :], v_ref[src, :]
        stg[0, rows, :] = qv
        stg[1, rows, :] = kv
        stg[2, rows, :] = vv
        put_block(nblk + g, padded + (g + 1) * KEYS_BACK, qv, kv, vv)
    for g in range(nblk):
        src = d16_rows_in_d4_order(g)
        put_block(2 * nblk + g, 2 * padded + (g + 1) * KEYS_BACK,
                  stg[0, src, :], stg[1, src, :], stg[2, src, :])

    def out_rows(j):
        if j // nblk == 2:
            r = j % nblk
            return pl.ds(2 * seq + (r % d4) * quarter + r // d4, KEYS_BACK, stride=d4)
        return pl.ds(j * KEYS_BACK, KEYS_BACK)

    def key_rows(j):
        return pl.ds((j // nblk) * padded + (j % nblk) * KEYS_BACK, 2 * KEYS_BACK)

    def qk(j):
        return _dot_nt(qs[j], kp[key_rows(j), :])

    def softmax(j, s):
        di, g = j // nblk, j % nblk
        no_prev = g % (nblk // DILATIONS[di]) == 0
        s = s + bt[2 * di + int(no_prev)]
        m = jnp.max(s, axis=-1, keepdims=True)
        mb[out_rows(j), :] = jnp.where(first_head, m[:KEYS_BACK], m[KEYS_BACK:])
        return jnp.exp2(s - m).astype(BF16)

    def pv(j, p):
        r = _dot(p, va[key_rows(j), :])
        ab[out_rows(j), :] = jnp.where(first_head, r[:KEYS_BACK, :LANES], r[KEYS_BACK:, :LANES])
        lb[out_rows(j), :] = jnp.where(first_head, r[:KEYS_BACK, LANES:], r[KEYS_BACK:, LANES:])

    n_blocks = len(DILATIONS) * nblk
    for j in range(n_blocks):
        pv(j, softmax(j, qk(j)))

    def combine(c, carry):
        tok = d4_rows(c)
        views = (tok,
                 pl.ds(pl.multiple_of(seq + c * KEYS_BACK, KEYS_BACK), KEYS_BACK),
                 pl.ds(pl.multiple_of(2 * seq + c * KEYS_BACK, KEYS_BACK), KEYS_BACK))
        ms = [mb[v, :] for v in views]
        mx = jnp.maximum(jnp.maximum(ms[0], ms[1]), ms[2])
        num = jnp.zeros((KEYS_BACK, LANES), F32)
        den = jnp.zeros((KEYS_BACK, LANES), F32)
        for m, v in zip(ms, views):
            e = jnp.exp2(m - mx)
            num = num + e * ab[v, :]
            den = den + e * lb[v, :]
        o_ref[tok, :] = num / den
        return carry

    lax.fori_loop(0, nblk, combine, 0)


def _attn_call(rel_bias, q, k, v, batch, seq):
    nd = len(DILATIONS)
    nblk = seq // KEYS_BACK
    n_pairs = B_HEADS // HEADS_PER_STEP
    io_spec = pl.BlockSpec((seq, LANES), lambda hp, b: (b, hp))
    return pl.pallas_call(
        _attn_kernel,
        grid=(n_pairs, batch),
        in_specs=[
            pl.BlockSpec(memory_space=pltpu.SMEM),
            io_spec, io_spec, io_spec,
        ],
        out_specs=io_spec,
        out_shape=jax.ShapeDtypeStruct((batch * seq, B_WIDTH), F32),
        scratch_shapes=[
            pltpu.VMEM((nd * nblk, 2 * KEYS_BACK, LANES), BF16),
            pltpu.VMEM((nd * (seq + KEYS_BACK), LANES), BF16),
            pltpu.VMEM((nd * (seq + KEYS_BACK), 2 * LANES), BF16),
            pltpu.VMEM((2 * nd, 2 * KEYS_BACK, 2 * KEYS_BACK), F32),
            pltpu.VMEM((3, seq, LANES), F32),
            pltpu.VMEM((nd * seq, LANES), F32),
            pltpu.VMEM((nd * seq, LANES), F32),
            pltpu.VMEM((nd * seq, LANES), F32),
        ],
        compiler_params=pltpu.CompilerParams(
            dimension_semantics=("arbitrary", "arbitrary"), vmem_limit_bytes=VMEM_LIMIT),
        name="dilated_attn",
    )(rel_bias, q, k, v)


def _ffn_kernel(x_ref, a_ref, b_ref, wo_ref, gpost_ref, gpre2_ref, wg_ref, wu_ref, cw_ref, cb_ref,
                wd_ref, gpost2_ref, o_ref, gbuf, carry, act_buf, *, tiles_per_seq):
    tm = x_ref.shape[0]
    halo = carry.shape[1]
    n_chunks = wd_ref.shape[0] // FF_CHUNK
    first = pl.program_id(0) % tiles_per_seq == 0

    @pl.when(first)
    def _():
        carry[...] = jnp.zeros(carry.shape, F32)

    y1 = (_dot(a_ref[...], wo_ref[0:A_WIDTH, :])
          + _dot(b_ref[...].astype(BF16), wo_ref[A_WIDTH:, :]))
    x1 = x_ref[...] + _rms(y1, gpost_ref[...])
    h2 = _rms(x1, gpre2_ref[...]).astype(BF16)

    def gate_up(c):
        cols = slice(c * FF_CHUNK, (c + 1) * FF_CHUNK)
        return _dot(h2, wg_ref[:, cols]), _dot(h2, wu_ref[:, cols])

    gu = gate_up(0)
    for c in range(n_chunks):
        cols = slice(c * FF_CHUNK, (c + 1) * FF_CHUNK)
        gu_next = gate_up(c + 1) if c + 1 < n_chunks else None
        g, up = gu
        gbuf[0:halo, :] = carry[c]
        gbuf[halo:halo + tm, :] = g
        carry[c] = g[tm - halo:, :]
        cw = cw_ref[:, cols]
        cv = (cb_ref[:, cols]
              + gbuf[halo - 2:halo - 2 + tm, :] * cw[0:1]
              + gbuf[halo - 1:halo - 1 + tm, :] * cw[1:2]
              + g * cw[2:3])
        act_buf[:, cols] = (jax.nn.gelu(cv) * up).astype(BF16)
        gu = gu_next
    o_ref[...] = x1 + _rms(_dot(act_buf[...], wd_ref[...]), gpost2_ref[...])


def _ffn_call(x2, a, b, wo, gpost, gpre2, wg, wu, cw, cb, wd, gpost2, seq):
    n, d = x2.shape
    tm = FFN_TOKENS_PER_STEP
    ff = wd.shape[0]
    halo = SUBLANES
    const = lambda i: (0, 0)
    tile = lambda w: pl.BlockSpec((tm, w), lambda i: (i, 0))
    once = pl.Buffered(1)
    return pl.pallas_call(
        functools.partial(_ffn_kernel, tiles_per_seq=seq // tm),
        grid=(n // tm,),
        in_specs=[
            tile(d), tile(A_WIDTH), tile(B_WIDTH),
            pl.BlockSpec((d, d), const, pipeline_mode=once),
            pl.BlockSpec((1, d), const),
            pl.BlockSpec((1, d), const),
            pl.BlockSpec((d, ff), const, pipeline_mode=once),
            pl.BlockSpec((d, ff), const, pipeline_mode=once),
            pl.BlockSpec((CONV_WIDTH, ff), const),
            pl.BlockSpec((1, ff), const),
            pl.BlockSpec((ff, d), const, pipeline_mode=once),
            pl.BlockSpec((1, d), const),
        ],
        out_specs=tile(d),
        out_shape=jax.ShapeDtypeStruct((n, d), F32),
        scratch_shapes=[
            pltpu.VMEM((halo + tm, FF_CHUNK), F32),
            pltpu.VMEM((ff // FF_CHUNK, halo, FF_CHUNK), F32),
            pltpu.VMEM((tm, ff), BF16),
        ],
        compiler_params=pltpu.CompilerParams(
            dimension_semantics=("arbitrary",), vmem_limit_bytes=VMEM_LIMIT),
        name="outproj_ffn",
    )(x2, a, b, wo, gpost, gpre2, wg, wu, cw, cb, wd, gpost2)


def kernel(x, norm_mix_pre, norm_mix_post, norm_ffn_pre, norm_ffn_post, w_in, ln_v_gain, ln_v_bias,
           spatial_w, spatial_b, rel_bias, w_out, w_gate, w_up, conv_w, conv_b, w_down):
    batch, seq, d = x.shape
    depth = w_in.shape[0]
    ff = w_gate.shape[-1]
    assert seq % PROJ_TOKENS_PER_STEP == 0 and seq % FFN_TOKENS_PER_STEP == 0
    assert ff % FF_CHUNK == 0 and d == A_WIDTH + B_WIDTH
    assert seq // KEYS_BACK == max(DILATIONS)

    gmat = jnp.asarray(np.kron(np.eye(A_GROUPS), np.ones((HEAD_DIM, HEAD_DIM))), BF16)
    x2 = x.reshape(batch * seq, d)
    for l in range(depth):
        row = lambda p: p[l].reshape(1, -1).astype(F32)
        sb = jnp.repeat(spatial_b[l].astype(F32).T, HEAD_DIM, axis=1)
        later = [w[l].astype(F32) for w in (w_out, w_gate, w_up, w_down)]
        a_out, q, k, v, wo_bf, wg_bf, wu_bf, wd_bf = _proj_call(
            x2, row(norm_mix_pre), w_in[l].astype(F32), gmat,
            row(ln_v_gain), row(ln_v_bias), spatial_w[l].astype(F32), sb, later)
        b_out = _attn_call(rel_bias.astype(F32), q, k, v, batch, seq)
        x2 = _ffn_call(
            x2, a_out, b_out, wo_bf, row(norm_mix_post), row(norm_ffn_pre),
            wg_bf, wu_bf, conv_w[l].astype(F32), row(conv_b),
            wd_bf, row(norm_ffn_post), seq)
    return x2.reshape(batch, seq, d)
```

```python
import functools
import math

import numpy as np
import jax
import jax.numpy as jnp
from jax import lax
from jax.experimental import pallas as pl
from jax.experimental.pallas import tpu as pltpu

F32 = jnp.float32
BF16 = jnp.bfloat16

HEAD_DIM = 64
A_GROUPS = 4
A_WIDTH = A_GROUPS * HEAD_DIM
B_HEADS = 12
B_WIDTH = B_HEADS * HEAD_DIM
CHUNK = 128
DILATIONS = (1, 4, 16)
KEYS_BACK = 128
NUM_BUCKETS = 32
MAX_DISTANCE = 2048
CONV_WIDTH = 3
NORM_EPS = 1e-6
NEG_INF = -1e30
LOG2E = 1.4426950408889634

LANES = 128
SUBLANES = 8
HEADS_PER_STEP = LANES // HEAD_DIM
FF_CHUNK = 256
PROJ_TOKENS_PER_STEP = 1024
ROWS_PER_PASS = 512
FFN_TOKENS_PER_STEP = 1024
WEIGHT_ROWS_PER_COPY = 256
N_LATER_WEIGHTS = 4
VMEM_LIMIT = 60 * 1024 * 1024


def _rms(x, g):
    ms = jnp.mean(x * x, axis=-1, keepdims=True)
    return x * lax.rsqrt(ms + NORM_EPS) * g


def _dot(a, b):
    return jnp.dot(a, b, preferred_element_type=F32)


def _dot_nt(a, b):
    return lax.dot_general(a, b, (((1,), (1,)), ((), ())), preferred_element_type=F32)


def _load_weight_as_bf16(w_hbm, dst, stage, sems):
    rows = stage.shape[1]
    n_chunks = w_hbm.shape[0] // rows

    def chunk_copy(c):
        return pltpu.make_async_copy(
            w_hbm.at[pl.ds(c * rows, rows), :], stage.at[c % 2], sems.at[c % 2])

    chunk_copy(0).start()
    for c in range(n_chunks):
        if c + 1 < n_chunks:
            chunk_copy(c + 1).start()
        chunk_copy(c).wait()
        dst[pl.ds(c * rows, rows), :] = stage[c % 2].astype(BF16)


def _proj_kernel(x_ref, g_ref, w_hbm, gmat_ref, lng_ref, lnb_ref, sw_ref, sb_ref, *rest, wd_col_steps):
    later_f32, rest = rest[:N_LATER_WEIGHTS], rest[N_LATER_WEIGHTS:]
    a_ref, q_ref, k_ref, v_ref = rest[:4]
    later_bf16, (uv_ref, w_ref, w_stage, w_sems) = rest[4:4 + N_LATER_WEIGHTS], rest[4 + N_LATER_WEIGHTS:]

    @pl.when(pl.program_id(0) == 0)
    def _():
        _load_weight_as_bf16(w_hbm, w_ref, w_stage, w_sems)

    for src, dst in zip(later_f32[:-1], later_bf16[:-1]):
        dst[...] = src[...].astype(BF16)

    @pl.when(pl.program_id(0) < wd_col_steps)
    def _():
        later_bf16[-1][...] = later_f32[-1][...].astype(BF16)

    gmat = gmat_ref[...]
    row = lax.broadcasted_iota(jnp.int32, (CHUNK, CHUNK), 0)
    col = lax.broadcasted_iota(jnp.int32, (CHUNK, CHUNK), 1)
    tril = row >= col
    ws = [jnp.where(tril, sw_ref[g], 0.0).astype(BF16) for g in range(A_GROUPS)]
    grp2 = (lax.broadcasted_iota(jnp.int32, (CHUNK, 2 * A_WIDTH), 1) // HEAD_DIM) % A_GROUPS
    sb = sb_ref[...]
    o = 2 * A_WIDTH

    def group_sum(t):
        hi = t.astype(BF16)
        lo = (t - hi.astype(F32)).astype(BF16)
        return _dot(hi, gmat) + _dot(lo, gmat)

    n_pass = x_ref.shape[0] // ROWS_PER_PASS
    pass_rows = [pl.ds(p * ROWS_PER_PASS, ROWS_PER_PASS) for p in range(n_pass)]

    for rows in pass_rows:
        h = _rms(x_ref[rows, :], g_ref[...]).astype(BF16)
        q_ref[rows, :] = _dot(h, w_ref[:, o:o + B_WIDTH]) * (LOG2E / math.sqrt(HEAD_DIM))
        k_ref[rows, :] = _dot(h, w_ref[:, o + B_WIDTH:o + 2 * B_WIDTH])
        v_ref[rows, :] = _dot(h, w_ref[:, o + 2 * B_WIDTH:o + 3 * B_WIDTH])
        uv_ref[rows, :] = _dot(h, w_ref[:, 0:o])

    u = jax.nn.gelu(uv_ref[:, 0:A_WIDTH])
    vf = jax.nn.gelu(uv_ref[:, A_WIDTH:o])
    mu = group_sum(vf) * (1.0 / HEAD_DIM)
    dl = vf - mu
    var = group_sum(dl * dl) * (1.0 / HEAD_DIM)
    vn = (dl * lax.rsqrt(var + NORM_EPS) * lng_ref[...] + lnb_ref[...]).astype(BF16)

    for c in range(0, x_ref.shape[0] // CHUNK, 2):
        lo_rows = slice(c * CHUNK, (c + 1) * CHUNK)
        hi_rows = slice((c + 1) * CHUNK, (c + 2) * CHUNK)
        vc = jnp.concatenate([vn[lo_rows], vn[hi_rows]], axis=1)
        z = _dot(ws[0], vc)
        for g in range(1, A_GROUPS):
            z = jnp.where(grp2 == g, _dot(ws[g], vc), z)
        for half, chunk in enumerate((lo_rows, hi_rows)):
            zc = z[:, half * A_WIDTH:(half + 1) * A_WIDTH]
            a_ref[chunk, :] = (u[chunk] * (zc + sb)).astype(BF16)


def _proj_call(x2, g_pre, w_in, gmat, lng, lnb, sw, sb, later_weights):
    n, d = x2.shape
    tm = PROJ_TOKENS_PER_STEP
    cols = w_in.shape[1]
    steps = n // tm
    const = lambda i: (0, 0)
    once = pl.Buffered(1)
    assert len(later_weights) == N_LATER_WEIGHTS
    *by_rows, w_down = later_weights
    slices = [pl.BlockSpec((w.shape[0] // steps, w.shape[1]), lambda i: (i, 0)) for w in by_rows]
    assert all(w.shape[0] % (steps * 2 * SUBLANES) == 0 for w in by_rows)
    wd_col_steps = w_down.shape[1] // LANES
    assert wd_col_steps <= steps
    slices.append(pl.BlockSpec((w_down.shape[0], LANES),
                               lambda i: (0, jnp.minimum(i, wd_col_steps - 1))))
    return pl.pallas_call(
        functools.partial(_proj_kernel, wd_col_steps=wd_col_steps),
        grid=(n // tm,),
        in_specs=[
            pl.BlockSpec((tm, d), lambda i: (i, 0)),
            pl.BlockSpec((1, d), const, pipeline_mode=once),
            pl.BlockSpec(memory_space=pl.ANY),
            pl.BlockSpec((A_WIDTH, A_WIDTH), const, pipeline_mode=once),
            pl.BlockSpec((1, A_WIDTH), const, pipeline_mode=once),
            pl.BlockSpec((1, A_WIDTH), const, pipeline_mode=once),
            pl.BlockSpec((A_GROUPS, CHUNK, CHUNK), lambda i: (0, 0, 0), pipeline_mode=once),
            pl.BlockSpec((CHUNK, A_WIDTH), const, pipeline_mode=once),
        ] + slices,
        out_specs=[
            pl.BlockSpec((tm, A_WIDTH), lambda i: (i, 0)),
            pl.BlockSpec((tm, B_WIDTH), lambda i: (i, 0)),
            pl.BlockSpec((tm, B_WIDTH), lambda i: (i, 0)),
            pl.BlockSpec((tm, B_WIDTH), lambda i: (i, 0)),
        ] + slices,
        out_shape=[
            jax.ShapeDtypeStruct((n, A_WIDTH), BF16),
            jax.ShapeDtypeStruct((n, B_WIDTH), F32),
            jax.ShapeDtypeStruct((n, B_WIDTH), F32),
            jax.ShapeDtypeStruct((n, B_WIDTH), F32),
        ] + [jax.ShapeDtypeStruct(w.shape, BF16) for w in later_weights],
        scratch_shapes=[
            pltpu.VMEM((tm, 2 * A_WIDTH), F32),
            pltpu.VMEM((d, cols), BF16),
            pltpu.VMEM((2, WEIGHT_ROWS_PER_COPY, cols), F32),
            pltpu.SemaphoreType.DMA((2,)),
        ],
        compiler_params=pltpu.CompilerParams(
            dimension_semantics=("arbitrary",), vmem_limit_bytes=VMEM_LIMIT),
        name="proj_gmlp",
    )(x2, g_pre, w_in, gmat, lng, lnb, sw, sb, *later_weights)


def _t5_bucket_np(dist):
    max_exact = NUM_BUCKETS // 2
    d = np.maximum(dist, 1).astype(np.float64)
    large = max_exact + (np.log(d / max_exact) / math.log(MAX_DISTANCE / max_exact)
                         * (NUM_BUCKETS - max_exact))
    large = np.minimum(large.astype(np.int32), NUM_BUCKETS - 1)
    return np.where(dist < max_exact, dist, large)


def _bucket_starts(d):
    b = _t5_bucket_np(np.arange(KEYS_BACK + 1) * d)
    return [(r, int(b[r])) for r in range(KEYS_BACK + 1) if r == 0 or b[r] != b[r - 1]]


def _attn_kernel(rb_ref, q_ref, k_ref, v_ref, o_ref,
                 qs, kp, va, bt, stg, mb, lb, ab):
    seq = q_ref.shape[0]
    nblk = seq // KEYS_BACK
    padded = seq + KEYS_BACK
    assert DILATIONS == (1, 4, 16) and nblk == DILATIONS[-1]
    lane = lax.broadcasted_iota(jnp.int32, (KEYS_BACK, LANES), 1)
    first_head = lane < HEAD_DIM

    @pl.when(pl.program_id(1) == 0)
    def _():
        i = lax.broadcasted_iota(jnp.int32, (KEYS_BACK, 2 * KEYS_BACK), 0)
        j = lax.broadcasted_iota(jnp.int32, (KEYS_BACK, 2 * KEYS_BACK), 1)
        rel = KEYS_BACK + i - j
        band = (rel >= 0) & (rel <= KEYS_BACK)
        band_cur = band & (j >= KEYS_BACK)
        for di, d in enumerate(DILATIONS):
            for hh in range(HEADS_PER_STEP):
                head = pl.program_id(0) * HEADS_PER_STEP + hh
                val = jnp.zeros(rel.shape, F32)
                for start, bucket in _bucket_starts(d):
                    val = jnp.where(rel >= start, rb_ref[bucket, head], val)
                val = val * LOG2E
                rows = pl.ds(hh * KEYS_BACK, KEYS_BACK)
                bt[2 * di, rows, :] = jnp.where(band, val, NEG_INF)
                bt[2 * di + 1, rows, :] = jnp.where(band_cur, val, NEG_INF)

    d4 = DILATIONS[1]
    quarter = seq // d4

    def d4_rows(c):
        return pl.ds(c // d4 + (c % d4) * KEYS_BACK * d4, KEYS_BACK, stride=d4)

    def d16_rows_in_d4_order(r):
        return pl.ds((r % d4) * quarter + r // d4, KEYS_BACK, stride=d4)

    def put_block(j, key_row, qv, kv, vv):
        qs[j, 0:KEYS_BACK, :] = jnp.where(first_head, qv, 0.0).astype(BF16)
        qs[j, KEYS_BACK:, :] = jnp.where(first_head, 0.0, qv).astype(BF16)
        kp[pl.ds(key_row, KEYS_BACK), :] = kv.astype(BF16)
        va[pl.ds(key_row, KEYS_BACK), 0:LANES] = vv.astype(BF16)

    @pl.when((pl.program_id(0) == 0) & (pl.program_id(1) == 0))
    def _():
        for di in range(len(DILATIONS)):
            kp[pl.ds(di * padded, KEYS_BACK), :] = jnp.zeros((KEYS_BACK, LANES), BF16)
            va[pl.ds(di * padded, KEYS_BACK), :] = jnp.zeros((KEYS_BACK, 2 * LANES), BF16)
            va[pl.ds(di * padded + KEYS_BACK, seq), LANES:] = jnp.ones((seq, LANES), BF16)

    for g in range(nblk):
        rows = pl.ds(g * KEYS_BACK, KEYS_BACK)
        put_block(g, (g + 1) * KEYS_BACK, q_ref[rows, :], k_ref[rows, :], v_ref[rows, :])
    for g in range(nblk):
        src = d4_rows(g)
        rows = pl.ds(g * KEYS_BACK, KEYS_BACK)
        qv, kv, vv = q_ref[src, :], k_ref[src, :], v_ref[src, :]
        stg[0, rows, :] = qv
        stg[1, rows, :] = kv
        stg[2, rows, :] = vv
        put_block(nblk + g, padded + (g + 1) * KEYS_BACK, qv, kv, vv)
    for g in range(nblk):
        src = d16_rows_in_d4_order(g)
        put_block(2 * nblk + g, 2 * padded + (g + 1) * KEYS_BACK,
                  stg[0, src, :], stg[1, src, :], stg[2, src, :])

    def out_rows(j):
        if j // nblk == 2:
            r = j % nblk
            return pl.ds(2 * seq + (r % d4) * quarter + r // d4, KEYS_BACK, stride=d4)
        return pl.ds(j * KEYS_BACK, KEYS_BACK)

    def key_rows(j):
        return pl.ds((j // nblk) * padded + (j % nblk) * KEYS_BACK, 2 * KEYS_BACK)

    def qk(j):
        return _dot_nt(qs[j], kp[key_rows(j), :])

    def softmax(j, s):
        di, g = j // nblk, j % nblk
        no_prev = g % (nblk // DILATIONS[di]) == 0
        s = s + bt[2 * di + int(no_prev)]
        m = jnp.max(s, axis=-1, keepdims=True)
        mb[out_rows(j), :] = jnp.where(first_head, m[:KEYS_BACK], m[KEYS_BACK:])
        return jnp.exp2(s - m).astype(BF16)

    def pv(j, p):
        r = _dot(p, va[key_rows(j), :])
        ab[out_rows(j), :] = jnp.where(first_head, r[:KEYS_BACK, :LANES], r[KEYS_BACK:, :LANES])
        lb[out_rows(j), :] = jnp.where(first_head, r[:KEYS_BACK, LANES:], r[KEYS_BACK:, LANES:])

    n_blocks = len(DILATIONS) * nblk
    for j in range(n_blocks):
        pv(j, softmax(j, qk(j)))

    def combine(c, carry):
        tok = d4_rows(c)
        views = (tok,
                 pl.ds(pl.multiple_of(seq + c * KEYS_BACK, KEYS_BACK), KEYS_BACK),
                 pl.ds(pl.multiple_of(2 * seq + c * KEYS_BACK, KEYS_BACK), KEYS_BACK))
        ms = [mb[v, :] for v in views]
        mx = jnp.maximum(jnp.maximum(ms[0], ms[1]), ms[2])
        num = jnp.zeros((KEYS_BACK, LANES), F32)
        den = jnp.zeros((KEYS_BACK, LANES), F32)
        for m, v in zip(ms, views):
            e = jnp.exp2(m - mx)
            num = num + e * ab[v, :]
            den = den + e * lb[v, :]
        o_ref[tok, :] = num / den
        return carry

    lax.fori_loop(0, nblk, combine, 0)


def _attn_call(rel_bias, q, k, v, batch, seq):
    nd = len(DILATIONS)
    nblk = seq // KEYS_BACK
    n_pairs = B_HEADS // HEADS_PER_STEP
    io_spec = pl.BlockSpec((seq, LANES), lambda hp, b: (b, hp))
    return pl.pallas_call(
        _attn_kernel,
        grid=(n_pairs, batch),
        in_specs=[
            pl.BlockSpec(memory_space=pltpu.SMEM),
            io_spec, io_spec, io_spec,
        ],
        out_specs=io_spec,
        out_shape=jax.ShapeDtypeStruct((batch * seq, B_WIDTH), F32),
        scratch_shapes=[
            pltpu.VMEM((nd * nblk, 2 * KEYS_BACK, LANES), BF16),
            pltpu.VMEM((nd * (seq + KEYS_BACK), LANES), BF16),
            pltpu.VMEM((nd * (seq + KEYS_BACK), 2 * LANES), BF16),
            pltpu.VMEM((2 * nd, 2 * KEYS_BACK, 2 * KEYS_BACK), F32),
            pltpu.VMEM((3, seq, LANES), F32),
            pltpu.VMEM((nd * seq, LANES), F32),
            pltpu.VMEM((nd * seq, LANES), F32),
            pltpu.VMEM((nd * seq, LANES), F32),
        ],
        compiler_params=pltpu.CompilerParams(
            dimension_semantics=("arbitrary", "arbitrary"), vmem_limit_bytes=VMEM_LIMIT),
        name="dilated_attn",
    )(rel_bias, q, k, v)


def _ffn_kernel(x_ref, a_ref, b_ref, wo_ref, gpost_ref, gpre2_ref, wg_ref, wu_ref, cw_ref, cb_ref,
                wd_ref, gpost2_ref, o_ref, gbuf, carry, act_buf, *, tiles_per_seq):
    tm = x_ref.shape[0]
    halo = carry.shape[1]
    n_chunks = wd_ref.shape[0] // FF_CHUNK
    first = pl.program_id(0) % tiles_per_seq == 0

    @pl.when(first)
    def _():
        carry[...] = jnp.zeros(carry.shape, F32)

    y1 = (_dot(a_ref[...], wo_ref[0:A_WIDTH, :])
          + _dot(b_ref[...].astype(BF16), wo_ref[A_WIDTH:, :]))
    x1 = x_ref[...] + _rms(y1, gpost_ref[...])
    h2 = _rms(x1, gpre2_ref[...]).astype(BF16)

    def gate_up(c):
        cols = slice(c * FF_CHUNK, (c + 1) * FF_CHUNK)
        return _dot(h2, wg_ref[:, cols]), _dot(h2, wu_ref[:, cols])

    gu = gate_up(0)
    for c in range(n_chunks):
        cols = slice(c * FF_CHUNK, (c + 1) * FF_CHUNK)
        gu_next = gate_up(c + 1) if c + 1 < n_chunks else None
        g, up = gu
        gbuf[0:halo, :] = carry[c]
        gbuf[halo:halo + tm, :] = g
        carry[c] = g[tm - halo:, :]
        cw = cw_ref[:, cols]
        cv = (cb_ref[:, cols]
              + gbuf[halo - 2:halo - 2 + tm, :] * cw[0:1]
              + gbuf[halo - 1:halo - 1 + tm, :] * cw[1:2]
              + g * cw[2:3])
        act_buf[:, cols] = (jax.nn.gelu(cv) * up).astype(BF16)
        gu = gu_next
    o_ref[...] = x1 + _rms(_dot(act_buf[...], wd_ref[...]), gpost2_ref[...])


def _ffn_call(x2, a, b, wo, gpost, gpre2, wg, wu, cw, cb, wd, gpost2, seq):
    n, d = x2.shape
    tm = FFN_TOKENS_PER_STEP
    ff = wd.shape[0]
    halo = SUBLANES
    const = lambda i: (0, 0)
    tile = lambda w: pl.BlockSpec((tm, w), lambda i: (i, 0))
    once = pl.Buffered(1)
    return pl.pallas_call(
        functools.partial(_ffn_kernel, tiles_per_seq=seq // tm),
        grid=(n // tm,),
        in_specs=[
            tile(d), tile(A_WIDTH), tile(B_WIDTH),
            pl.BlockSpec((d, d), const, pipeline_mode=once),
            pl.BlockSpec((1, d), const),
            pl.BlockSpec((1, d), const),
            pl.BlockSpec((d, ff), const, pipeline_mode=once),
            pl.BlockSpec((d, ff), const, pipeline_mode=once),
            pl.BlockSpec((CONV_WIDTH, ff), const),
            pl.BlockSpec((1, ff), const),
            pl.BlockSpec((ff, d), const, pipeline_mode=once),
            pl.BlockSpec((1, d), const),
        ],
        out_specs=tile(d),
        out_shape=jax.ShapeDtypeStruct((n, d), F32),
        scratch_shapes=[
            pltpu.VMEM((halo + tm, FF_CHUNK), F32),
            pltpu.VMEM((ff // FF_CHUNK, halo, FF_CHUNK), F32),
            pltpu.VMEM((tm, ff), BF16),
        ],
        compiler_params=pltpu.CompilerParams(
            dimension_semantics=("arbitrary",), vmem_limit_bytes=VMEM_LIMIT),
        name="outproj_ffn",
    )(x2, a, b, wo, gpost, gpre2, wg, wu, cw, cb, wd, gpost2)


def kernel(x, norm_mix_pre, norm_mix_post, norm_ffn_pre, norm_ffn_post, w_in, ln_v_gain, ln_v_bias,
           spatial_w, spatial_b, rel_bias, w_out, w_gate, w_up, conv_w, conv_b, w_down):
    batch, seq, d = x.shape
    depth = w_in.shape[0]
    ff = w_gate.shape[-1]
    assert seq % PROJ_TOKENS_PER_STEP == 0 and seq % FFN_TOKENS_PER_STEP == 0
    assert ff % FF_CHUNK == 0 and d == A_WIDTH + B_WIDTH
    assert seq // KEYS_BACK == max(DILATIONS)

    gmat = jnp.asarray(np.kron(np.eye(A_GROUPS), np.ones((HEAD_DIM, HEAD_DIM))), BF16)
    x2 = x.reshape(batch * seq, d)
    for l in range(depth):
        row = lambda p: p[l].reshape(1, -1).astype(F32)
        sb = jnp.repeat(spatial_b[l].astype(F32).T, HEAD_DIM, axis=1)
        later = [w[l].astype(F32) for w in (w_out, w_gate, w_up, w_down)]
        a_out, q, k, v, wo_bf, wg_bf, wu_bf, wd_bf = _proj_call(
            x2, row(norm_mix_pre), w_in[l].astype(F32), gmat,
            row(ln_v_gain), row(ln_v_bias), spatial_w[l].astype(F32), sb, later)
        b_out = _attn_call(rel_bias.astype(F32), q, k, v, batch, seq)
        x2 = _ffn_call(
            x2, a_out, b_out, wo_bf, row(norm_mix_post), row(norm_ffn_pre),
            wg_bf, wu_bf, conv_w[l].astype(F32), row(conv_b),
            wd_bf, row(norm_ffn_post), seq)
    return x2.reshape(batch, seq, d)
```

```python
import functools
import math

import numpy as np
import jax
import jax.numpy as jnp
from jax import lax
from jax.experimental import pallas as pl
from jax.experimental.pallas import tpu as pltpu

F32 = jnp.float32
BF16 = jnp.bfloat16

HEAD_DIM = 64
A_GROUPS = 4
A_WIDTH = A_GROUPS * HEAD_DIM
B_HEADS = 12
B_WIDTH = B_HEADS * HEAD_DIM
CHUNK = 128
DILATIONS = (1, 4, 16)
KEYS_BACK = 128
NUM_BUCKETS = 32
MAX_DISTANCE = 2048
CONV_WIDTH = 3
NORM_EPS = 1e-6
NEG_INF = -1e30
LOG2E = 1.4426950408889634

LANES = 128
HEADS_PER_STEP = LANES // HEAD_DIM
FF_CHUNK = 256
PROJ_TOKENS_PER_STEP = 1024
ROWS_PER_PASS = 512
FFN_TOKENS_PER_STEP = 512
VMEM_LIMIT = 56 * 1024 * 1024


def _rms(x, g):
    ms = jnp.mean(x * x, axis=-1, keepdims=True)
    return x * lax.rsqrt(ms + NORM_EPS) * g


def _dot(a, b):
    return jnp.dot(a, b, preferred_element_type=F32)


def _dot_nt(a, b):
    return lax.dot_general(a, b, (((1,), (1,)), ((), ())), preferred_element_type=F32)


def _proj_kernel(x_ref, g_ref, w_ref, gmat_ref, lng_ref, lnb_ref, sw_ref, sb_ref,
                 a_ref, *rest):
    nat_refs, d4_refs, d16_refs = rest[0:3], rest[3:6], rest[6:9]
    uv_ref, tok_f32, d4_f32 = rest[9:12]
    gmat = gmat_ref[...]
    row = lax.broadcasted_iota(jnp.int32, (CHUNK, CHUNK), 0)
    col = lax.broadcasted_iota(jnp.int32, (CHUNK, CHUNK), 1)
    tril = row >= col
    ws = [jnp.where(tril, sw_ref[g], 0.0).astype(BF16) for g in range(A_GROUPS)]
    grp = lax.broadcasted_iota(jnp.int32, (CHUNK, A_WIDTH), 1) // HEAD_DIM
    sb = sb_ref[...]
    o = 2 * A_WIDTH
    n_res = DILATIONS[1]
    per_res = ROWS_PER_PASS // n_res
    per_res16 = ROWS_PER_PASS // DILATIONS[2]

    def group_sum(t):
        hi = t.astype(BF16)
        lo = (t - hi.astype(F32)).astype(BF16)
        return _dot(hi, gmat) + _dot(lo, gmat)

    n_pass = x_ref.shape[0] // ROWS_PER_PASS
    pass_rows = [pl.ds(p * ROWS_PER_PASS, ROWS_PER_PASS) for p in range(n_pass)]

    for p, rows in enumerate(pass_rows):
        h = _rms(x_ref[rows, :], g_ref[...]).astype(BF16)
        for n in range(3):
            val = _dot(h, w_ref[:, o + n * B_WIDTH:o + (n + 1) * B_WIDTH])
            if n == 0:
                val = val * (LOG2E / math.sqrt(HEAD_DIM))
            nat_refs[n][rows, :] = val.astype(BF16)
            for c in range(B_WIDTH // LANES):
                lanes = slice(c * LANES, (c + 1) * LANES)
                tok_f32[c] = val[:, lanes]
                for r in range(n_res):
                    piece = tok_f32[c, pl.ds(r, per_res, stride=n_res), :]
                    d4_f32[c, pl.ds(r * per_res, per_res), :] = piece
                    d4_refs[n][r, pl.ds(p * per_res, per_res), lanes] = piece.astype(BF16)
                for r16 in range(DILATIONS[2]):
                    src = pl.ds((r16 % n_res) * per_res + r16 // n_res, per_res16, stride=n_res)
                    d16_refs[n][r16, pl.ds(p * per_res16, per_res16), lanes] = (
                        d4_f32[c, src, :].astype(BF16))
        uv_ref[rows, :] = _dot(h, w_ref[:, 0:o])

    for rows in pass_rows:
        u = jax.nn.gelu(uv_ref[rows, 0:A_WIDTH])
        vf = jax.nn.gelu(uv_ref[rows, A_WIDTH:o])
        mu = group_sum(vf) * (1.0 / HEAD_DIM)
        dl = vf - mu
        var = group_sum(dl * dl) * (1.0 / HEAD_DIM)
        vn = (dl * lax.rsqrt(var + NORM_EPS) * lng_ref[...] + lnb_ref[...]).astype(BF16)

        for c in range(ROWS_PER_PASS // CHUNK):
            chunk = slice(c * CHUNK, (c + 1) * CHUNK)
            vc = vn[chunk]
            z = _dot(ws[0], vc)
            for g in range(1, A_GROUPS):
                z = jnp.where(grp == g, _dot(ws[g], vc), z)
            a_ref[pl.ds(rows.start + c * CHUNK, CHUNK), :] = (u[chunk] * (z + sb)).astype(BF16)


def _proj_call(x2, g_pre, w_in, gmat, lng, lnb, sw, sb, batch, seq):
    n, d = x2.shape
    tm = PROJ_TOKENS_PER_STEP
    cols = w_in.shape[1]
    steps_per_seq = seq // tm
    const = lambda i: (0, 0)
    once = pl.Buffered(1)

    def regrouped(dil):
        shape = (batch, dil, seq // dil, B_WIDTH)
        spec = pl.BlockSpec((None, dil, tm // dil, B_WIDTH),
                            lambda i: (i // steps_per_seq, 0, i % steps_per_seq, 0))
        return jax.ShapeDtypeStruct(shape, BF16), spec

    nat = (jax.ShapeDtypeStruct((n, B_WIDTH), BF16), pl.BlockSpec((tm, B_WIDTH), lambda i: (i, 0)))
    outs = [nat] * 3 + [regrouped(DILATIONS[1])] * 3 + [regrouped(DILATIONS[2])] * 3
    return pl.pallas_call(
        _proj_kernel,
        grid=(n // tm,),
        in_specs=[
            pl.BlockSpec((tm, d), lambda i: (i, 0)),
            pl.BlockSpec((1, d), const, pipeline_mode=once),
            pl.BlockSpec((d, cols), const, pipeline_mode=once),
            pl.BlockSpec((A_WIDTH, A_WIDTH), const, pipeline_mode=once),
            pl.BlockSpec((1, A_WIDTH), const, pipeline_mode=once),
            pl.BlockSpec((1, A_WIDTH), const, pipeline_mode=once),
            pl.BlockSpec((A_GROUPS, CHUNK, CHUNK), lambda i: (0, 0, 0), pipeline_mode=once),
            pl.BlockSpec((CHUNK, A_WIDTH), const, pipeline_mode=once),
        ],
        out_specs=[pl.BlockSpec((tm, A_WIDTH), lambda i: (i, 0))] + [spec for _, spec in outs],
        out_shape=[jax.ShapeDtypeStruct((n, A_WIDTH), BF16)] + [shape for shape, _ in outs],
        scratch_shapes=[
            pltpu.VMEM((tm, 2 * A_WIDTH), F32),
            pltpu.VMEM((B_WIDTH // LANES, ROWS_PER_PASS, LANES), F32),
            pltpu.VMEM((B_WIDTH // LANES, ROWS_PER_PASS, LANES), F32),
        ],
        compiler_params=pltpu.CompilerParams(
            dimension_semantics=("arbitrary",), vmem_limit_bytes=VMEM_LIMIT),
        name="proj_gmlp",
    )(x2, g_pre, w_in, gmat, lng, lnb, sw, sb)


def _t5_bucket_np(dist):
    max_exact = NUM_BUCKETS // 2
    d = np.maximum(dist, 1).astype(np.float64)
    large = max_exact + (np.log(d / max_exact) / math.log(MAX_DISTANCE / max_exact)
                         * (NUM_BUCKETS - max_exact))
    large = np.minimum(large.astype(np.int32), NUM_BUCKETS - 1)
    return np.where(dist < max_exact, dist, large)


def _bucket_starts(d):
    b = _t5_bucket_np(np.arange(KEYS_BACK + 1) * d)
    return [(r, int(b[r])) for r in range(KEYS_BACK + 1) if r == 0 or b[r] != b[r - 1]]


def _attn_kernel(rb_ref, qn_ref, kn_ref, vn_ref, q4_ref, k4_ref, v4_ref, q16_ref, k16_ref, v16_ref,
                 o_ref, qs, va, bt, mb, lb, ab):
    seq = qn_ref.shape[0]
    nblk = seq // KEYS_BACK
    n_blocks = len(DILATIONS) * nblk
    n_res = DILATIONS[1]
    quarter = seq // n_res
    assert DILATIONS == (1, 4, 16) and nblk == DILATIONS[-1]
    lane = lax.broadcasted_iota(jnp.int32, (KEYS_BACK, LANES), 1)
    first_head = lane < HEAD_DIM

    @pl.when(pl.program_id(1) == 0)
    def _():
        i = lax.broadcasted_iota(jnp.int32, (KEYS_BACK, 2 * KEYS_BACK), 0)
        j = lax.broadcasted_iota(jnp.int32, (KEYS_BACK, 2 * KEYS_BACK), 1)
        for di, d in enumerate(DILATIONS):
            for hh in range(HEADS_PER_STEP):
                head = pl.program_id(0) * HEADS_PER_STEP + hh
                rows = pl.ds(hh * KEYS_BACK, KEYS_BACK)
                for variant, rel in enumerate((KEYS_BACK + i - j, i - j)):
                    val = jnp.zeros(rel.shape, F32)
                    for start, bucket in _bucket_starts(d):
                        val = jnp.where(rel >= start, rb_ref[bucket, head], val)
                    band = (rel >= 0) & (rel <= KEYS_BACK)
                    bt[2 * di + variant, rows, :] = jnp.where(band, val * LOG2E, NEG_INF)

    def block_rows(j):
        return pl.ds(j * KEYS_BACK, KEYS_BACK)

    @pl.when((pl.program_id(0) == 0) & (pl.program_id(1) == 0))
    def _():
        va[:, 0:LANES] = jnp.zeros((va.shape[0], LANES), BF16)
        va[:, LANES:] = jnp.ones((va.shape[0], LANES), BF16)

    def is_first(j):
        return (j % nblk) % (nblk // DILATIONS[j // nblk]) == 0

    def source(j, refs):
        di, g = j // nblk, j % nblk
        if di == 0:
            return refs[0][block_rows(g), :]
        if di == 1:
            return refs[1][g // n_res, block_rows(g % n_res), :]
        return refs[2][g]

    for j in range(n_blocks):
        qb = source(j, (qn_ref, q4_ref, q16_ref))
        zero = jnp.zeros_like(qb)
        qs[j, 0:KEYS_BACK, :] = jnp.where(first_head, qb, zero)
        qs[j, KEYS_BACK:, :] = jnp.where(first_head, zero, qb)
        va[block_rows(j), 0:LANES] = source(j, (vn_ref, v4_ref, v16_ref))

    def out_rows(j):
        if j // nblk == 2:
            r = j % nblk
            return pl.ds(2 * seq + (r % n_res) * quarter + r // n_res, KEYS_BACK, stride=n_res)
        return block_rows(j)

    def keys_of(j):
        k_refs = (kn_ref, k4_ref, k16_ref)
        if is_first(j):
            nxt = j + 1 if (j + 1) // nblk == j // nblk else j
            return jnp.concatenate([source(j, k_refs), source(nxt, k_refs)], axis=0)
        return jnp.concatenate([source(j - 1, k_refs), source(j, k_refs)], axis=0)

    def attend(j):
        first = is_first(j)
        s = _dot_nt(qs[j], keys_of(j)) + bt[2 * (j // nblk) + int(first)]
        m = jnp.max(s, axis=-1, keepdims=True)
        mb[out_rows(j), :] = jnp.where(first_head, m[:KEYS_BACK], m[KEYS_BACK:])
        p = jnp.exp2(s - m).astype(BF16)
        vals = va[pl.ds((j if first else j - 1) * KEYS_BACK, 2 * KEYS_BACK), :]
        r = _dot(p, vals)
        ab[out_rows(j), :] = jnp.where(first_head, r[:KEYS_BACK, :LANES], r[KEYS_BACK:, :LANES])
        lb[out_rows(j), :] = jnp.where(first_head, r[:KEYS_BACK, LANES:], r[KEYS_BACK:, LANES:])

    for j in range(n_blocks):
        attend(j)

    def combine(c, carry):
        tok = pl.ds(c // n_res + (c % n_res) * (KEYS_BACK * n_res), KEYS_BACK, stride=n_res)
        views = (tok,
                 pl.ds(pl.multiple_of((nblk + c) * KEYS_BACK, KEYS_BACK), KEYS_BACK),
                 pl.ds(pl.multiple_of((2 * nblk + c) * KEYS_BACK, KEYS_BACK), KEYS_BACK))
        ms = [mb[v, :] for v in views]
        mx = jnp.maximum(jnp.maximum(ms[0], ms[1]), ms[2])
        num = jnp.zeros((KEYS_BACK, LANES), F32)
        den = jnp.zeros((KEYS_BACK, LANES), F32)
        for m, v in zip(ms, views):
            e = jnp.exp2(m - mx)
            num = num + e * ab[v, :]
            den = den + e * lb[v, :]
        o_ref[tok, :] = num / den
        return carry

    lax.fori_loop(0, nblk, combine, 0)


def _attn_call(rel_bias, nat, by4, by16, batch, seq):
    nd = len(DILATIONS)
    nblk = seq // KEYS_BACK
    n_pairs = B_HEADS // HEADS_PER_STEP
    io_spec = pl.BlockSpec((seq, LANES), lambda hp, b: (b, hp))
    res_spec = lambda dil: pl.BlockSpec((None, dil, seq // dil, LANES), lambda hp, b: (b, 0, 0, hp))
    return pl.pallas_call(
        _attn_kernel,
        grid=(n_pairs, batch),
        in_specs=([pl.BlockSpec(memory_space=pltpu.SMEM)] + [io_spec] * 3
                  + [res_spec(DILATIONS[1])] * 3 + [res_spec(DILATIONS[2])] * 3),
        out_specs=io_spec,
        out_shape=jax.ShapeDtypeStruct((batch * seq, B_WIDTH), F32),
        scratch_shapes=[
            pltpu.VMEM((nd * nblk, 2 * KEYS_BACK, LANES), BF16),
            pltpu.VMEM((nd * seq + KEYS_BACK, 2 * LANES), BF16),
            pltpu.VMEM((2 * nd, 2 * KEYS_BACK, 2 * KEYS_BACK), F32),
            pltpu.VMEM((nd * seq, LANES), F32),
            pltpu.VMEM((nd * seq, LANES), F32),
            pltpu.VMEM((nd * seq, LANES), F32),
        ],
        compiler_params=pltpu.CompilerParams(
            dimension_semantics=("arbitrary", "arbitrary"), vmem_limit_bytes=VMEM_LIMIT),
        name="dilated_attn",
    )(rel_bias, *nat, *by4, *by16)


def _ffn_kernel(x_ref, a_ref, b_ref, wo_ref, gpost_ref, gpre2_ref, wg_ref, wu_ref, cw_ref, cb_ref,
                wd_ref, gpost2_ref, o_ref, gbuf, carry, act_buf, *, tiles_per_seq):
    tm = x_ref.shape[0]
    halo = carry.shape[1]
    n_chunks = wd_ref.shape[0] // FF_CHUNK
    first = pl.program_id(0) % tiles_per_seq == 0

    @pl.when(first)
    def _():
        carry[...] = jnp.zeros(carry.shape, F32)

    y1 = (_dot(a_ref[...], wo_ref[0:A_WIDTH, :])
          + _dot(b_ref[...].astype(BF16), wo_ref[A_WIDTH:, :]))
    x1 = x_ref[...] + _rms(y1, gpost_ref[...])
    h2 = _rms(x1, gpre2_ref[...]).astype(BF16)

    def gate_up(c):
        cols = slice(c * FF_CHUNK, (c + 1) * FF_CHUNK)
        return _dot(h2, wg_ref[:, cols]), _dot(h2, wu_ref[:, cols])

    gu = gate_up(0)
    for c in range(n_chunks):
        cols = slice(c * FF_CHUNK, (c + 1) * FF_CHUNK)
        gu_next = gate_up(c + 1) if c + 1 < n_chunks else None
        g, up = gu
        gbuf[0:halo, :] = carry[c]
        gbuf[halo:halo + tm, :] = g
        carry[c] = g[tm - halo:, :]
        cw = cw_ref[:, cols]
        cv = (cb_ref[:, cols]
              + gbuf[halo - 2:halo - 2 + tm, :] * cw[0:1]
              + gbuf[halo - 1:halo - 1 + tm, :] * cw[1:2]
              + g * cw[2:3])
        act_buf[:, cols] = (jax.nn.gelu(cv) * up).astype(BF16)
        gu = gu_next
    o_ref[...] = x1 + _rms(_dot(act_buf[...], wd_ref[...]), gpost2_ref[...])


def _ffn_call(x2, a, b, wo, gpost, gpre2, wg, wu, cw, cb, wd, gpost2, seq):
    n, d = x2.shape
    tm = FFN_TOKENS_PER_STEP
    ff = wd.shape[0]
    halo = 8
    const = lambda i: (0, 0)
    tile = lambda w: pl.BlockSpec((tm, w), lambda i: (i, 0))
    return pl.pallas_call(
        functools.partial(_ffn_kernel, tiles_per_seq=seq // tm),
        grid=(n // tm,),
        in_specs=[
            tile(d), tile(A_WIDTH), tile(B_WIDTH),
            pl.BlockSpec((d, d), const),
            pl.BlockSpec((1, d), const),
            pl.BlockSpec((1, d), const),
            pl.BlockSpec((d, ff), const),
            pl.BlockSpec((d, ff), const),
            pl.BlockSpec((CONV_WIDTH, ff), const),
            pl.BlockSpec((1, ff), const),
            pl.BlockSpec((ff, d), const),
            pl.BlockSpec((1, d), const),
        ],
        out_specs=tile(d),
        out_shape=jax.ShapeDtypeStruct((n, d), F32),
        scratch_shapes=[
            pltpu.VMEM((halo + tm, FF_CHUNK), F32),
            pltpu.VMEM((ff // FF_CHUNK, halo, FF_CHUNK), F32),
            pltpu.VMEM((tm, ff), BF16),
        ],
        compiler_params=pltpu.CompilerParams(
            dimension_semantics=("arbitrary",), vmem_limit_bytes=VMEM_LIMIT),
        name="outproj_ffn",
    )(x2, a, b, wo, gpost, gpre2, wg, wu, cw, cb, wd, gpost2)


def kernel(x, norm_mix_pre, norm_mix_post, norm_ffn_pre, norm_ffn_post, w_in, ln_v_gain, ln_v_bias,
           spatial_w, spatial_b, rel_bias, w_out, w_gate, w_up, conv_w, conv_b, w_down):
    batch, seq, d = x.shape
    depth = w_in.shape[0]
    ff = w_gate.shape[-1]
    assert seq % PROJ_TOKENS_PER_STEP == 0 and seq % FFN_TOKENS_PER_STEP == 0
    assert ff % FF_CHUNK == 0 and d == A_WIDTH + B_WIDTH
    assert seq // KEYS_BACK == max(DILATIONS)

    gmat = jnp.asarray(np.kron(np.eye(A_GROUPS), np.ones((HEAD_DIM, HEAD_DIM))), BF16)
    x2 = x.reshape(batch * seq, d)
    for l in range(depth):
        row = lambda p: p[l].reshape(1, -1).astype(F32)
        sb = jnp.repeat(spatial_b[l].astype(F32).T, HEAD_DIM, axis=1)
        a_out, *qkv = _proj_call(
            x2, row(norm_mix_pre), w_in[l].astype(BF16), gmat,
            row(ln_v_gain), row(ln_v_bias), spatial_w[l].astype(F32), sb, batch, seq)
        b_out = _attn_call(rel_bias.astype(F32), qkv[0:3], qkv[3:6], qkv[6:9], batch, seq)
        x2 = _ffn_call(
            x2, a_out, b_out, w_out[l].astype(BF16), row(norm_mix_post), row(norm_ffn_pre),
            w_gate[l].astype(BF16), w_up[l].astype(BF16), conv_w[l].astype(F32), row(conv_b),
            w_down[l].astype(BF16), row(norm_ffn_post), seq)
    return x2.reshape(batch, seq, d)
```
